```python
import jax, jax.numpy as jnp
from jax import lax
import numpy as np

D_MODEL = 1024
BATCH = 16
SEQ = 2048
DEPTH = 1

MEM_LEN = 256
MEM_HEADS = 4
MEM_DIM = 128
MLA_HEADS = 8
MLA_NOPE = 64
MLA_ROPE = 32
MLA_V = 64
Q_LORA = 384
KV_LORA = 256
ROPE_THETA = 10000.0
SB_HEADS = 8
SB_DIM = 64
D_FF = 4 * D_MODEL
N_BRANCH = 3
BRANCH_W = 512
Q_BLOCK = 128
EPS = 1e-6

IN_SIZES = [Q_LORA, KV_LORA, MLA_ROPE, 3 * SB_HEADS * SB_DIM, MEM_HEADS * MEM_DIM, N_BRANCH * D_MODEL]
IN_WIDTH = int(sum(IN_SIZES))
IN_SPLITS = [int(v) for v in np.cumsum(IN_SIZES)[:-1]]

kernel_name = "hybrid_mla_stickbreak_memxattn_gated"


def rms_norm(x, g):
    xf = x.astype(jnp.float32)
    y = xf * lax.rsqrt(jnp.mean(xf * xf, axis=-1, keepdims=True) + EPS)
    return (y * g.astype(jnp.float32)).astype(x.dtype)


def apply_rope(x, positions):
    half = MLA_ROPE // 2
    inv_freq = 1.0 / (ROPE_THETA ** (jnp.arange(half, dtype=jnp.float32) * (2.0 / MLA_ROPE)))
    ang = positions.astype(jnp.float32)[..., None] * inv_freq
    cos = jnp.cos(ang)[:, :, None, :]
    sin = jnp.sin(ang)[:, :, None, :]
    xf = x.astype(jnp.float32)
    x1, x2 = xf[..., :half], xf[..., half:]
    out = jnp.concatenate([x1 * cos - x2 * sin, x1 * sin + x2 * cos], axis=-1)
    return out.astype(x.dtype)


def causal_softmax_attention(q, k, v, scale):
    b, s, h, _ = q.shape
    outs = []
    for i in range(s // Q_BLOCK):
        q0, q1 = i * Q_BLOCK, (i + 1) * Q_BLOCK
        qb, kb, vb = q[:, q0:q1], k[:, :q1], v[:, :q1]
        sc = jnp.einsum('bqhd,bkhd->bhqk', qb, kb).astype(jnp.float32) * scale
        t_idx = jnp.arange(q0, q1)[:, None]
        s_idx = jnp.arange(q1)[None, :]
        sc = jnp.where(t_idx >= s_idx, sc, jnp.finfo(jnp.float32).min)
        p = jax.nn.softmax(sc, axis=-1).astype(v.dtype)
        outs.append(jnp.einsum('bhqk,bkhd->bqhd', p, vb))
    o = jnp.concatenate(outs, axis=1)
    return o.reshape(b, s, h * v.shape[-1])


def stick_breaking_attention(q, k, v, scale):
    b, s, h, _ = q.shape
    outs = []
    for i in range(s // Q_BLOCK):
        q0, q1 = i * Q_BLOCK, (i + 1) * Q_BLOCK
        qb, kb, vb = q[:, q0:q1], k[:, :q1], v[:, :q1]
        z = jnp.einsum('bqhd,bkhd->bhqk', qb, kb).astype(jnp.float32) * scale
        t_idx = jnp.arange(q0, q1)[:, None]
        s_idx = jnp.arange(q1)[None, :]
        strict = t_idx > s_idx
        log_keep = jnp.where(strict, jax.nn.log_sigmoid(-z), 0.0)
        rev_incl = lax.cumsum(log_keep, axis=3, reverse=True)
        rev_excl = jnp.concatenate([rev_incl[..., 1:], jnp.zeros_like(rev_incl[..., :1])], axis=-1)
        log_a = jax.nn.log_sigmoid(z) + rev_excl
        a = jnp.where(strict, jnp.exp(log_a), 0.0).astype(v.dtype)
        outs.append(jnp.einsum('bhqk,bkhd->bqhd', a, vb))
    o = jnp.concatenate(outs, axis=1)
    return o.reshape(b, s, h * v.shape[-1])


def memory_cross_attention(q, k, v, scale):
    b, s, h, d = q.shape
    sc = jnp.einsum('bshd,bmhd->bhsm', q, k).astype(jnp.float32) * scale
    p = jax.nn.softmax(sc, axis=-1).astype(v.dtype)
    return jnp.einsum('bhsm,bmhd->bshd', p, v).reshape(b, s, h * d)


def setup_inputs(seed: int = 0) -> dict:
    key = jax.random.key(seed)
    ks = jax.random.split(key, 24)
    f32 = jnp.float32

    def w(k, shape, fan_in):
        return jax.random.normal(k, shape, f32) * (fan_in ** -0.5)

    def gain(k, shape):
        return 1.0 + 0.02 * jax.random.normal(k, shape, f32)

    x = jax.random.normal(ks[0], (BATCH, SEQ, D_MODEL), f32)
    mem = jax.random.normal(ks[1], (BATCH, MEM_LEN, D_MODEL), f32)
    offsets = jax.random.randint(ks[2], (BATCH, 1), 0, 4096, dtype=jnp.int32)
    positions = (jnp.arange(SEQ, dtype=jnp.int32)[None, :] + offsets).astype(jnp.int32)
    return {
        "x": x,
        "mem": mem,
        "positions": positions,
        "ln_mix_pre": gain(ks[3], (DEPTH, D_MODEL)),
        "w_in": w(ks[4], (DEPTH, D_MODEL, IN_WIDTH), D_MODEL),
        "b_gate": 0.01 * jax.random.normal(ks[5], (DEPTH, N_BRANCH * D_MODEL), f32),
        "q_norm": gain(ks[6], (DEPTH, Q_LORA)),
        "w_uq": w(ks[7], (DEPTH, Q_LORA, MLA_HEADS * (MLA_NOPE + MLA_ROPE)), Q_LORA),
        "kv_norm": gain(ks[8], (DEPTH, KV_LORA)),
        "w_uk": w(ks[9], (DEPTH, KV_LORA, MLA_HEADS * MLA_NOPE), KV_LORA),
        "w_uv": w(ks[10], (DEPTH, KV_LORA, MLA_HEADS * MLA_V), KV_LORA),
        "mem_norm": gain(ks[11], (DEPTH, D_MODEL)),
        "w_mem_kv": w(ks[12], (DEPTH, D_MODEL, 2 * MEM_HEADS * MEM_DIM), D_MODEL),
        "w_branch_out": w(ks[13], (DEPTH, N_BRANCH, BRANCH_W, D_MODEL), BRANCH_W),
        "w_out": w(ks[14], (DEPTH, D_MODEL, D_MODEL), D_MODEL),
        "ln_mix_post": gain(ks[15], (DEPTH, D_MODEL)),
        "ln_mlp_pre": gain(ks[16], (DEPTH, D_MODEL)),
        "w_mlp_up": w(ks[17], (DEPTH, D_MODEL, D_FF), D_MODEL),
        "w_mlp_down": w(ks[18], (DEPTH, D_FF, D_MODEL), D_FF),
        "ln_mlp_post": gain(ks[19], (DEPTH, D_MODEL)),
    }


def reference(x, mem, positions, ln_mix_pre, w_in, b_gate, q_norm, w_uq, kv_norm, w_uk, w_uv,
              mem_norm, w_mem_kv, w_branch_out, w_out, ln_mix_post, ln_mlp_pre, w_mlp_up,
              w_mlp_down, ln_mlp_post):
    b, s, d = x.shape
    m = mem.shape[1]
    for l in range(DEPTH):
        h = rms_norm(x, ln_mix_pre[l])
        proj = jnp.einsum('bsd,de->bse', h, w_in[l])
        c_q, c_kv, k_r, sb_qkv, q_m, gate_logits = jnp.split(proj, IN_SPLITS, axis=-1)

        c_q = rms_norm(c_q, q_norm[l])
        q = jnp.einsum('bsr,re->bse', c_q, w_uq[l]).reshape(b, s, MLA_HEADS, MLA_NOPE + MLA_ROPE)
        q_nope, q_rope = q[..., :MLA_NOPE], apply_rope(q[..., MLA_NOPE:], positions)
        c_kv = rms_norm(c_kv, kv_norm[l])
        k_nope = jnp.einsum('bsr,re->bse', c_kv, w_uk[l]).reshape(b, s, MLA_HEADS, MLA_NOPE)
        v_mla = jnp.einsum('bsr,re->bse', c_kv, w_uv[l]).reshape(b, s, MLA_HEADS, MLA_V)
        k_rope = apply_rope(k_r[:, :, None, :], positions)
        q_full = jnp.concatenate([q_nope, q_rope], axis=-1)
        k_full = jnp.concatenate([k_nope, jnp.broadcast_to(k_rope, (b, s, MLA_HEADS, MLA_ROPE))], axis=-1)
        o_mla = causal_softmax_attention(q_full, k_full, v_mla, (MLA_NOPE + MLA_ROPE) ** -0.5)

        sq, sk, sv = jnp.split(sb_qkv.reshape(b, s, 3, SB_HEADS, SB_DIM), 3, axis=2)
        o_sb = stick_breaking_attention(sq[:, :, 0], sk[:, :, 0], sv[:, :, 0], SB_DIM ** -0.5)

        mem_h = rms_norm(mem, mem_norm[l])
        mkv = jnp.einsum('bmd,de->bme', mem_h, w_mem_kv[l]).reshape(b, m, 2, MEM_HEADS, MEM_DIM)
        o_mem = memory_cross_attention(q_m.reshape(b, s, MEM_HEADS, MEM_DIM), mkv[:, :, 0], mkv[:, :, 1],
                                       MEM_DIM ** -0.5)

        gates = jax.nn.sigmoid(gate_logits.astype(jnp.float32) + b_gate[l].astype(jnp.float32))
        gates = gates.reshape(b, s, N_BRANCH, d).astype(x.dtype)
        merged = None
        for i, o in enumerate((o_mla, o_sb, o_mem)):
            yb = gates[:, :, i] * jnp.einsum('bsc,cd->bsd', o, w_branch_out[l, i])
            merged = yb if merged is None else merged + yb
        y = jnp.einsum('bsd,de->bse', merged, w_out[l])
        x = x + rms_norm(y, ln_mix_post[l])

        h = rms_norm(x, ln_mlp_pre[l])
        u = jnp.square(jax.nn.relu(jnp.einsum('bsd,df->bsf', h, w_mlp_up[l])))
        x = x + rms_norm(jnp.einsum('bsf,fd->bsd', u, w_mlp_down[l]), ln_mlp_post[l])
    return x
```

```python
import functools

import jax
import jax.numpy as jnp
import numpy as np
from jax import lax
from jax.experimental import pallas as pl
from jax.experimental.pallas import tpu as pltpu

F32 = jnp.float32
BF16 = jnp.bfloat16

LANES = 128
EPS = 1e-6
ROPE_THETA = 10000.0

MLA_HEADS, MLA_NOPE, MLA_ROPE, MLA_V = 8, 64, 32, 64
SB_HEADS, SB_DIM = 8, 64
MEM_HEADS, MEM_DIM = 4, 128
N_BRANCH = 3
NEG_BIG = -1e30

ROW_TILE = 512
Q_TILE = 256
K_TILE = 256
FF_CHUNK = 1024
VMEM_LIMIT = 60 * 1024 * 1024


def _rms(x, g):
    return x * lax.rsqrt(jnp.mean(x * x, axis=-1, keepdims=True) + EPS) * g


def _dot(a, b):
    return jnp.dot(a, b, preferred_element_type=F32)


def _dot_t(a, b):
    return lax.dot_general(a, b, (((1,), (1,)), ((), ())), preferred_element_type=F32)


def _const_spec(shape):
    nd = len(shape)
    return pl.BlockSpec(shape, lambda *_: (0,) * nd, pipeline_mode=pl.Buffered(1))


def _mem_kv_kernel(mem_ref, g_ref, w_ref, o_ref):
    h = _rms(mem_ref[...], g_ref[...]).astype(BF16)
    o_ref[...] = _dot(h, w_ref[...]).astype(BF16)


def _mem_kv(mem2d, g, w):
    rows, d = mem2d.shape
    n = w.shape[1]
    tile = 512
    return pl.pallas_call(
        _mem_kv_kernel,
        grid=(rows // tile,),
        in_specs=[pl.BlockSpec((tile, d), lambda i: (i, 0)), _const_spec((1, d)), _const_spec((d, n))],
        out_specs=pl.BlockSpec((tile, n), lambda i: (i, 0)),
        out_shape=jax.ShapeDtypeStruct((rows, n), BF16),
        compiler_params=pltpu.CompilerParams(dimension_semantics=("parallel",), vmem_limit_bytes=VMEM_LIMIT),
        name="mem_kv",
    )(mem2d, g, w)


_CQ, _CKV, _SB, _QM, _KR = 384, 256, 3 * SB_HEADS * SB_DIM, MEM_HEADS * MEM_DIM, LANES
_O_CKV = _CQ
_O_SB = _O_CKV + _CKV
_O_QM = _O_SB + _SB
_O_KR = _O_QM + _QM
_O_KRR = _O_KR + _KR
_W1 = _O_KRR + _KR


def _in_proj_kernel(x_ref, pos_ref, freq_ref, g_ref, w1_ref, qn_ref, wuq_ref, wuqr_ref, kvn_ref, wuk_ref,
                    wuv_ref, mkv_ref, q_out, k_out, v_out, sq_out, sk_out, sv_out, om_out):
    h = _rms(x_ref[...], g_ref[...]).astype(BF16)
    proj = _dot(h, w1_ref[...])

    ang = pos_ref[...].astype(F32) * freq_ref[...]
    cos, sin = jnp.cos(ang), jnp.sin(ang)

    cq = _rms(proj[:, :_CQ], qn_ref[...]).astype(BF16)
    q = _dot(cq, wuq_ref[...])
    qr = _dot(cq, wuqr_ref[...])
    ckv = _rms(proj[:, _O_CKV:_O_SB], kvn_ref[...]).astype(BF16)
    kn = _dot(ckv, wuk_ref[...])
    v_out[...] = _dot(ckv, wuv_ref[...]).astype(BF16)
    kr = proj[:, _O_KR:_O_KRR] * cos + proj[:, _O_KRR:_W1] * sin
    q_scale = (MLA_NOPE + MLA_ROPE) ** -0.5
    for hd in range(MLA_HEADS):
        sl = slice(hd * LANES, (hd + 1) * LANES)
        q_out[:, sl] = ((q[:, sl] * cos + qr[:, sl] * sin) * q_scale).astype(BF16)
        k_out[:, sl] = (kn[:, sl] + kr).astype(BF16)

    w = SB_HEADS * SB_DIM
    sq_out[...] = (proj[:, _O_SB:_O_SB + w] * (SB_DIM ** -0.5)).astype(BF16)
    sk_out[...] = proj[:, _O_SB + w:_O_SB + 2 * w].astype(BF16)
    sv_out[...] = proj[:, _O_SB + 2 * w:_O_QM].astype(BF16)

    for hd in range(MEM_HEADS):
        sl = slice(hd * MEM_DIM, (hd + 1) * MEM_DIM)
        vsl = slice(_QM + hd * MEM_DIM, _QM + (hd + 1) * MEM_DIM)
        qm = (proj[:, _O_QM + hd * MEM_DIM:_O_QM + (hd + 1) * MEM_DIM] * (MEM_DIM ** -0.5)).astype(BF16)
        s = _dot_t(qm, mkv_ref[:, sl])
        p = jnp.exp(s - jnp.max(s, axis=-1, keepdims=True))
        o = _dot(p.astype(BF16), mkv_ref[:, vsl])
        om_out[:, sl] = (o / jnp.sum(p, axis=-1, keepdims=True)).astype(BF16)


def _in_proj(x2d, pos2d, freq, g, w1, qn, wuq, wuqr, kvn, wuk, wuv, mkv, seq):
    rows, d = x2d.shape
    tile = ROW_TILE
    per_seq = seq // tile
    mem_len = mkv.shape[1]
    row = lambda c: pl.BlockSpec((tile, c), lambda i: (i, 0))
    widths = (MLA_HEADS * LANES, MLA_HEADS * LANES, MLA_HEADS * MLA_V, SB_HEADS * SB_DIM, SB_HEADS * SB_DIM,
              SB_HEADS * SB_DIM, MEM_HEADS * MEM_DIM)
    mem_spec = pl.BlockSpec((None, mem_len, 2 * _QM), lambda i: (i // per_seq, 0, 0))
    return pl.pallas_call(
        _in_proj_kernel,
        grid=(rows // tile,),
        in_specs=[row(d), row(1), _const_spec(freq.shape), _const_spec(g.shape), _const_spec(w1.shape),
                  _const_spec(qn.shape), _const_spec(wuq.shape), _const_spec(wuqr.shape), _const_spec(kvn.shape),
                  _const_spec(wuk.shape), _const_spec(wuv.shape), mem_spec],
        out_specs=[row(c) for c in widths],
        out_shape=[jax.ShapeDtypeStruct((rows, c), BF16) for c in widths],
        compiler_params=pltpu.CompilerParams(dimension_semantics=("parallel",), vmem_limit_bytes=VMEM_LIMIT),
        name="in_proj",
    )(x2d, pos2d, freq, g, w1, qn, wuq, wuqr, kvn, wuk, wuv, mkv)


def _pair_select(first, a, b):
    return jnp.where(first, a, b)


def _mla_kernel(q_ref, k_ref, v_ref, o_ref):
    i = pl.program_id(2)
    first = lax.broadcasted_iota(jnp.int32, (Q_TILE, LANES), 1) < MLA_V
    qs = (q_ref[:, :LANES], q_ref[:, LANES:])

    def block(j, carry, diag):
        ms, ls, acc = carry
        kb = k_ref[pl.ds(pl.multiple_of(j * K_TILE, K_TILE), K_TILE), :]
        vb = v_ref[pl.ds(pl.multiple_of(j * K_TILE, K_TILE), K_TILE), :]
        new_m, new_l, alphas, pvs = [], [], [], []
        for hh in range(2):
            s = _dot_t(qs[hh], kb[:, hh * LANES:(hh + 1) * LANES])
            if diag:
                r = lax.broadcasted_iota(jnp.int32, s.shape, 0)
                c = lax.broadcasted_iota(jnp.int32, s.shape, 1)
                s = jnp.where(r >= c, s, NEG_BIG)
            m = jnp.maximum(ms[hh], jnp.max(s, axis=-1, keepdims=True))
            alpha = jnp.exp(ms[hh] - m)
            p = jnp.exp(s - m)
            new_m.append(m)
            new_l.append(alpha * ls[hh] + jnp.sum(p, axis=-1, keepdims=True))
            alphas.append(alpha)
            pvs.append(_dot(p.astype(BF16), vb))
        acc = _pair_select(first, alphas[0], alphas[1]) * acc + _pair_select(first, pvs[0], pvs[1])
        return tuple(new_m), tuple(new_l), acc

    col = jnp.full((Q_TILE, 1), NEG_BIG, F32)
    zero = jnp.zeros((Q_TILE, 1), F32)
    carry = ((col, col), (zero, zero), jnp.zeros((Q_TILE, LANES), F32))
    carry = lax.fori_loop(0, i, lambda j, c: block(j, c, False), carry)
    _, ls, acc = block(i, carry, True)
    o_ref[...] = (acc / _pair_select(first, ls[0], ls[1])).astype(BF16)


def _sb_kernel(q_ref, k_ref, v_ref, o_ref):
    i = pl.program_id(2)
    first = lax.broadcasted_iota(jnp.int32, (Q_TILE, LANES), 1) < SB_DIM
    qp = q_ref[...]
    zq = jnp.zeros_like(qp)
    qs = (jnp.where(first, qp, zq), jnp.where(first, zq, qp))
    r = lax.broadcasted_iota(jnp.int32, (K_TILE, K_TILE), 0)
    c = lax.broadcasted_iota(jnp.int32, (K_TILE, K_TILE), 1)
    later = jnp.where(r > c, 1.0, 0.0).astype(BF16)
    strict = r > c

    def block(j, carry, diag):
        cs, acc = carry
        kb = k_ref[pl.ds(pl.multiple_of(j * K_TILE, K_TILE), K_TILE), :]
        vb = v_ref[pl.ds(pl.multiple_of(j * K_TILE, K_TILE), K_TILE), :]
        new_c, pvs = [], []
        for hh in range(2):
            z = _dot_t(qs[hh], kb)
            sp = jnp.log1p(jnp.exp(-jnp.abs(z)))
            log_beta = jnp.minimum(z, 0.0) - sp
            log_keep = -(jnp.maximum(z, 0.0) + sp)
            if diag:
                log_keep = jnp.where(strict, log_keep, 0.0)
            right = _dot(log_keep.astype(BF16), later)
            a = jnp.exp(log_beta + right + cs[hh])
            if diag:
                a = jnp.where(strict, a, 0.0)
            pvs.append(_dot(a.astype(BF16), vb))
            new_c.append(cs[hh] + jnp.sum(log_keep, axis=-1, keepdims=True))
        return tuple(new_c), acc + _pair_select(first, pvs[0], pvs[1])

    zero = jnp.zeros((Q_TILE, 1), F32)
    carry = block(i, ((zero, zero), jnp.zeros((Q_TILE, LANES), F32)), True)
    _, acc = lax.fori_loop(0, i, lambda t, cr: block(i - 1 - t, cr, False), carry)
    o_ref[...] = acc.astype(BF16)


def _causal_mixer(body, name, q, k, v, qw, kw, vw):
    b, s, _ = q.shape
    pairs = v.shape[2] // vw
    return pl.pallas_call(
        body,
        grid=(b, pairs, s // Q_TILE),
        in_specs=[pl.BlockSpec((None, Q_TILE, qw), lambda bi, g, i: (bi, i, g)),
                  pl.BlockSpec((None, s, kw), lambda bi, g, i: (bi, 0, g)),
                  pl.BlockSpec((None, s, vw), lambda bi, g, i: (bi, 0, g))],
        out_specs=pl.BlockSpec((None, Q_TILE, vw), lambda bi, g, i: (bi, i, g)),
        out_shape=jax.ShapeDtypeStruct((b, s, pairs * vw), BF16),
        compiler_params=pltpu.CompilerParams(dimension_semantics=("parallel", "parallel", "arbitrary"),
                                             vmem_limit_bytes=VMEM_LIMIT),
        name=name,
    )(q, k, v)


def _post_kernel(x_ref, oa_ref, ob_ref, om_ref, g_pre_ref, wg_ref, bg_ref, wbo_ref, wout_ref, g_post_ref,
                 g_mlp_ref, wup_ref, wdn_ref, g_mlp_post_ref, o_ref):
    x = x_ref[...]
    d = x.shape[1]
    h = _rms(x, g_pre_ref[...]).astype(BF16)
    merged = None
    for br, o_br in enumerate((oa_ref, ob_ref, om_ref)):
        sl = slice(br * d, (br + 1) * d)
        gate = jax.nn.sigmoid(_dot(h, wg_ref[:, sl]) + bg_ref[:, sl])
        yb = gate * _dot(o_br[...], wbo_ref[br])
        merged = yb if merged is None else merged + yb
    y = _dot(merged.astype(BF16), wout_ref[...])
    x1 = x + _rms(y, g_post_ref[...])

    h2 = _rms(x1, g_mlp_ref[...]).astype(BF16)
    down = None
    for c0 in range(0, wup_ref.shape[1], FF_CHUNK):
        u = jnp.maximum(_dot(h2, wup_ref[:, c0:c0 + FF_CHUNK]), 0.0)
        part = _dot((u * u).astype(BF16), wdn_ref[c0:c0 + FF_CHUNK, :])
        down = part if down is None else down + part
    o_ref[...] = x1 + _rms(down, g_mlp_post_ref[...])


def _post(x2d, oa, ob, om, g_pre, wg, bg, wbo, wout, g_post, g_mlp, wup, wdn, g_mlp_post):
    rows, d = x2d.shape
    tile = ROW_TILE
    row = lambda c: pl.BlockSpec((tile, c), lambda i: (i, 0))
    consts = (g_pre, wg, bg, wbo, wout, g_post, g_mlp, wup, wdn, g_mlp_post)
    return pl.pallas_call(
        _post_kernel,
        grid=(rows // tile,),
        in_specs=[row(d), row(oa.shape[1]), row(ob.shape[1]), row(om.shape[1])] + [_const_spec(a.shape) for a in consts],
        out_specs=row(d),
        out_shape=jax.ShapeDtypeStruct((rows, d), F32),
        compiler_params=pltpu.CompilerParams(dimension_semantics=("parallel",), vmem_limit_bytes=VMEM_LIMIT),
        name="post",
    )(x2d, oa, ob, om, *consts)


def _pad_heads(w, heads, width):
    r = w.shape[0]
    w = w.reshape(r, heads, width)
    return jnp.pad(w, ((0, 0), (0, 0), (0, LANES - width))).reshape(r, heads * LANES)


def _rotate_half_cols(w_rope):
    half = MLA_ROPE // 2
    return jnp.concatenate([-w_rope[..., half:], w_rope[..., :half]], axis=-1)


def _rope_slot(w_rope):
    r, heads, _ = w_rope.shape
    return jnp.pad(w_rope, ((0, 0), (0, 0), (MLA_NOPE, LANES - MLA_NOPE - MLA_ROPE))).reshape(r, heads * LANES)


def kernel(x, mem, positions, ln_mix_pre, w_in, b_gate, q_norm, w_uq, kv_norm, w_uk, w_uv, mem_norm, w_mem_kv,
           w_branch_out, w_out, ln_mix_post, ln_mlp_pre, w_mlp_up, w_mlp_down, ln_mlp_post):
    b, s, d = x.shape
    m = mem.shape[1]
    depth = w_in.shape[0]
    half = MLA_ROPE // 2
    inv_freq = 1.0 / (ROPE_THETA ** (jnp.arange(half, dtype=F32) * (2.0 / MLA_ROPE)))
    freq = jnp.zeros((1, LANES), F32).at[0, MLA_NOPE:MLA_NOPE + MLA_ROPE].set(jnp.concatenate([inv_freq, inv_freq]))
    pos2d = positions.reshape(b * s, 1)
    mem2d = mem.reshape(b * m, d)
    x2d = x.reshape(b * s, d)
    row = lambda a: a.reshape(1, -1)

    for l in range(depth):
        cq_w, ckv_w, kr_w, sb_w, qm_w, gate_w = jnp.split(w_in[l], np.cumsum([_CQ, _CKV, MLA_ROPE, _SB, _QM]).tolist(), axis=1)
        kr3 = kr_w.reshape(d, 1, MLA_ROPE)
        w1 = jnp.concatenate([cq_w, ckv_w, sb_w, qm_w, _rope_slot(kr3), _rope_slot(_rotate_half_cols(kr3))], axis=1).astype(BF16)
        uq = w_uq[l].reshape(-1, MLA_HEADS, MLA_NOPE + MLA_ROPE)
        wuq = _pad_heads(w_uq[l], MLA_HEADS, MLA_NOPE + MLA_ROPE).astype(BF16)
        wuqr = _rope_slot(_rotate_half_cols(uq[..., MLA_NOPE:])).astype(BF16)
        wuk = _pad_heads(w_uk[l], MLA_HEADS, MLA_NOPE).astype(BF16)

        mkv = _mem_kv(mem2d, row(mem_norm[l]), w_mem_kv[l].astype(BF16))
        mkv = mkv.reshape(b, m, 2 * MEM_HEADS * MEM_DIM)

        q, k, v, sq, sk, sv, o_mem = _in_proj(
            x2d, pos2d, freq, row(ln_mix_pre[l]), w1, row(q_norm[l]), wuq, wuqr, row(kv_norm[l]), wuk,
            w_uv[l].astype(BF16), mkv, s)

        seq = lambda a: a.reshape(b, s, -1)
        o_mla = _causal_mixer(_mla_kernel, "mla_attn", seq(q), seq(k), seq(v), 2 * LANES, 2 * LANES, LANES)
        o_sb = _causal_mixer(_sb_kernel, "sb_attn", seq(sq), seq(sk), seq(sv), LANES, LANES, LANES)

        x2d = _post(x2d, o_mla.reshape(b * s, -1), o_sb.reshape(b * s, -1), o_mem, row(ln_mix_pre[l]),
                    gate_w.astype(BF16), row(b_gate[l]), w_branch_out[l].astype(BF16), w_out[l].astype(BF16),
                    row(ln_mix_post[l]), row(ln_mlp_pre[l]), w_mlp_up[l].astype(BF16), w_mlp_down[l].astype(BF16),
                    row(ln_mlp_post[l]))
    return x2d.reshape(b, s, d)
```

```python
import functools

import jax
import jax.numpy as jnp
import numpy as np
from jax import lax
from jax.experimental import pallas as pl
from jax.experimental.pallas import tpu as pltpu

F32 = jnp.float32
BF16 = jnp.bfloat16

LANES = 128
EPS = 1e-6
ROPE_THETA = 10000.0

MLA_HEADS, MLA_NOPE, MLA_ROPE, MLA_V = 8, 64, 32, 64
SB_HEADS, SB_DIM = 8, 64
MEM_HEADS, MEM_DIM = 4, 128
N_BRANCH = 3
NEG_BIG = -1e30

ROW_TILE = 512
Q_TILE = 512
SUB = 256
SUBS = Q_TILE // SUB
LOG2E = 1.4426950408889634
FF_CHUNK = 1024
VMEM_LIMIT = 60 * 1024 * 1024


def _rms(x, g):
    return x * lax.rsqrt(jnp.mean(x * x, axis=-1, keepdims=True) + EPS) * g


def _dot(a, b):
    return jnp.dot(a, b, preferred_element_type=F32)


def _dot_t(a, b):
    return lax.dot_general(a, b, (((1,), (1,)), ((), ())), preferred_element_type=F32)


def _const_spec(shape):
    nd = len(shape)
    return pl.BlockSpec(shape, lambda *_: (0,) * nd, pipeline_mode=pl.Buffered(1))


def _mem_kv_kernel(mem_ref, g_ref, w_ref, o_ref):
    h = _rms(mem_ref[...], g_ref[...]).astype(BF16)
    o_ref[...] = _dot(h, w_ref[...]).astype(BF16)


def _mem_kv(mem2d, g, w):
    rows, d = mem2d.shape
    n = w.shape[1]
    tile = 512
    return pl.pallas_call(
        _mem_kv_kernel,
        grid=(rows // tile,),
        in_specs=[pl.BlockSpec((tile, d), lambda i: (i, 0)), _const_spec((1, d)), _const_spec((d, n))],
        out_specs=pl.BlockSpec((tile, n), lambda i: (i, 0)),
        out_shape=jax.ShapeDtypeStruct((rows, n), BF16),
        compiler_params=pltpu.CompilerParams(dimension_semantics=("parallel",), vmem_limit_bytes=VMEM_LIMIT),
        name="mem_kv",
    )(mem2d, g, w)


_CQ, _CKV, _SB, _QM, _KR = 384, 256, 3 * SB_HEADS * SB_DIM, MEM_HEADS * MEM_DIM, LANES
_O_CKV = _CQ
_O_SB = _O_CKV + _CKV
_O_QM = _O_SB + _SB
_O_KR = _O_QM + _QM
_O_KRR = _O_KR + _KR
_W1 = _O_KRR + _KR


def _in_proj_kernel(x_ref, pos_ref, freq_ref, g_ref, w1_ref, qn_ref, wuq_ref, wuqr_ref, kvn_ref, wuk_ref,
                    wuv_ref, ones_ref, mkv_ref, q_out, k_out, v_out, sq_out, sk_out, sv_out, om_out):
    h = _rms(x_ref[...], g_ref[...]).astype(BF16)
    proj = _dot(h, w1_ref[...])

    ang = pos_ref[...].astype(F32) * freq_ref[...]
    cos, sin = jnp.cos(ang), jnp.sin(ang)

    cq = _rms(proj[:, :_CQ], qn_ref[...]).astype(BF16)
    q = _dot(cq, wuq_ref[...])
    qr = _dot(cq, wuqr_ref[...])
    ckv = _rms(proj[:, _O_CKV:_O_SB], kvn_ref[...]).astype(BF16)
    kn = _dot(ckv, wuk_ref[...])
    v_out[...] = (_dot(ckv, wuv_ref[...]) + ones_ref[...]).astype(BF16)
    kr = proj[:, _O_KR:_O_KRR] * cos + proj[:, _O_KRR:_W1] * sin
    q_scale = (MLA_NOPE + MLA_ROPE) ** -0.5 * LOG2E
    for hd in range(MLA_HEADS):
        sl = slice(hd * LANES, (hd + 1) * LANES)
        q_out[:, sl] = ((q[:, sl] * cos + qr[:, sl] * sin) * q_scale).astype(BF16)
        k_out[:, sl] = (kn[:, sl] + kr).astype(BF16)

    w = SB_HEADS * SB_DIM
    sq_out[...] = (proj[:, _O_SB:_O_SB + w] * (SB_DIM ** -0.5 * LOG2E)).astype(BF16)
    sk_out[...] = proj[:, _O_SB + w:_O_SB + 2 * w].astype(BF16)
    sv_out[...] = proj[:, _O_SB + 2 * w:_O_QM].astype(BF16)

    for hd in range(MEM_HEADS):
        sl = slice(hd * MEM_DIM, (hd + 1) * MEM_DIM)
        vsl = slice(_QM + hd * MEM_DIM, _QM + (hd + 1) * MEM_DIM)
        qm = (proj[:, _O_QM + hd * MEM_DIM:_O_QM + (hd + 1) * MEM_DIM] * (MEM_DIM ** -0.5)).astype(BF16)
        s = _dot_t(qm, mkv_ref[:, sl])
        p = jnp.exp(s - jnp.max(s, axis=-1, keepdims=True))
        o = _dot(p.astype(BF16), mkv_ref[:, vsl])
        om_out[:, sl] = (o / jnp.sum(p, axis=-1, keepdims=True)).astype(BF16)


def _in_proj(x2d, pos2d, freq, g, w1, qn, wuq, wuqr, kvn, wuk, wuv, ones, mkv, seq):
    rows, d = x2d.shape
    tile = ROW_TILE
    per_seq = seq // tile
    mem_len = mkv.shape[1]
    row = lambda c: pl.BlockSpec((tile, c), lambda i: (i, 0))
    widths = (MLA_HEADS * LANES, MLA_HEADS * LANES, MLA_HEADS * LANES, SB_HEADS * SB_DIM, SB_HEADS * SB_DIM,
              SB_HEADS * SB_DIM, MEM_HEADS * MEM_DIM)
    mem_spec = pl.BlockSpec((None, mem_len, 2 * _QM), lambda i: (i // per_seq, 0, 0))
    return pl.pallas_call(
        _in_proj_kernel,
        grid=(rows // tile,),
        in_specs=[row(d), row(1), _const_spec(freq.shape), _const_spec(g.shape), _const_spec(w1.shape),
                  _const_spec(qn.shape), _const_spec(wuq.shape), _const_spec(wuqr.shape), _const_spec(kvn.shape),
                  _const_spec(wuk.shape), _const_spec(wuv.shape), _const_spec(ones.shape), mem_spec],
        out_specs=[row(c) for c in widths],
        out_shape=[jax.ShapeDtypeStruct((rows, c), BF16) for c in widths],
        compiler_params=pltpu.CompilerParams(dimension_semantics=("parallel",), vmem_limit_bytes=VMEM_LIMIT),
        name="in_proj",
    )(x2d, pos2d, freq, g, w1, qn, wuq, wuqr, kvn, wuk, wuv, ones, mkv)


def _key_rows(ref, j, n_sub):
    return ref[pl.ds(pl.multiple_of(j * Q_TILE, Q_TILE), n_sub * SUB), :]


def _cat_rows(parts):
    return jnp.concatenate(parts, axis=0)


def _lane_fold_max(x):
    top = x[:, :LANES]
    for c0 in range(LANES, x.shape[1], LANES):
        top = jnp.maximum(top, x[:, c0:c0 + LANES])
    return jnp.max(top, axis=-1, keepdims=True)


def _mla_kernel(q_ref, k_ref, v_ref, o_ref):
    i = pl.program_id(2)
    heads = [slice(hh * LANES, (hh + 1) * LANES) for hh in range(2)]

    def tile(j, carry, diag):
        ms, accs = carry
        kb = _key_rows(k_ref, j, SUBS)
        vb = _key_rows(v_ref, j, SUBS)
        scores = [_dot_t(q_ref[:, hs], kb[:, hs]) for hs in heads]
        if diag:
            r = lax.broadcasted_iota(jnp.int32, (Q_TILE, Q_TILE), 0)
            c = lax.broadcasted_iota(jnp.int32, (Q_TILE, Q_TILE), 1)
            scores = [jnp.where(r >= c, s, NEG_BIG) for s in scores]
        new_m, new_acc = [], []
        for hh, hs in enumerate(heads):
            m = jnp.maximum(ms[hh], _lane_fold_max(scores[hh]))
            p = jnp.exp2(scores[hh] - m).astype(BF16)
            new_m.append(m)
            new_acc.append(jnp.exp2(ms[hh] - m) * accs[hh] + _dot(p, vb[:, hs]))
        return tuple(new_m), tuple(new_acc)

    col = jnp.full((Q_TILE, 1), NEG_BIG, F32)
    zero = jnp.zeros((Q_TILE, LANES), F32)
    carry = lax.fori_loop(0, i, lambda j, cr: tile(j, cr, False), ((col, col), (zero, zero)))
    _, (acc0, acc1) = tile(i, carry, True)
    first = lax.broadcasted_iota(jnp.int32, (Q_TILE, LANES), 1) < MLA_V
    half_turn = lambda x: pltpu.roll(x, MLA_V, 1)
    o_ref[...] = jnp.where(first, acc0 / half_turn(acc0), half_turn(acc1) / acc1).astype(BF16)


def _sb_kernel(q_ref, k_ref, v_ref, o_ref):
    assert SUBS == 2
    i = pl.program_id(2)
    first = lax.broadcasted_iota(jnp.int32, (Q_TILE, LANES), 1) < SB_DIM
    r = lax.broadcasted_iota(jnp.int32, (SUB, SUB), 0)
    c = lax.broadcasted_iota(jnp.int32, (SUB, SUB), 1)
    strict = r > c
    later = jnp.where(strict, 1.0, 0.0).astype(BF16)
    qp = q_ref[...]
    zq = jnp.zeros_like(qp)
    q2 = _cat_rows([jnp.where(first, qp, zq), jnp.where(first, zq, qp)])
    q2_low = _cat_rows([q2[SUB:Q_TILE], q2[Q_TILE + SUB:]])

    def log_terms(z):
        lg = jnp.log2(1.0 + jnp.exp2(-jnp.abs(z)))
        drop = lg - jnp.minimum(z, 0.0)
        return drop, z + drop

    def tile(j, carry, diag):
        tot, acc = carry
        kb = _key_rows(k_ref, j, SUBS)
        vb = _key_rows(v_ref, j, SUBS)
        if diag:
            drop1, keep1 = log_terms(_dot_t(q2_low, kb[SUB:]))
            drop0, keep0 = log_terms(_dot_t(q2, kb[:SUB]))
            rows_of = lambda n: lax.broadcasted_iota(jnp.int32, (n, SUB), 0)
            cols_of = lambda n: lax.broadcasted_iota(jnp.int32, (n, SUB), 1)
            strict2 = (rows_of(2 * SUB) & (SUB - 1)) > cols_of(2 * SUB)
            upper = (rows_of(2 * Q_TILE) & (Q_TILE - 1)) > cols_of(2 * Q_TILE)
            keep1 = jnp.where(strict2, keep1, 0.0)
            keep0 = jnp.where(upper, keep0, 0.0)
            right = _dot(_cat_rows([keep1, keep0]).astype(BF16), later)
            right1, right0 = right[:2 * SUB], right[2 * SUB:]
            low = lambda x: _cat_rows([x[SUB:Q_TILE], x[Q_TILE + SUB:]])
            w1 = jnp.where(strict2, jnp.exp2(-low(tot) - (drop1 + right1)), 0.0)
            sum1 = right1[:, 0:1] + keep1[:, 0:1]
            zeros = jnp.zeros((SUB, 1), F32)
            tot = tot + _cat_rows([zeros, sum1[:SUB], zeros, sum1[SUB:]])
            w0 = jnp.where(upper, jnp.exp2(-tot - (drop0 + right0)), 0.0)
            tot = tot + right0[:, 0:1] + keep0[:, 0:1]
            blank = jnp.zeros((SUB, SUB), BF16)
            w1 = w1.astype(BF16)
            w_right = _cat_rows([blank, w1[:SUB], blank, w1[SUB:]])
        else:
            drop, keep = log_terms(_dot_t(q2, kb))
            right = _dot(_cat_rows([keep[:, SUB:], keep[:, :SUB]]).astype(BF16), later)
            right1, right0 = right[:2 * Q_TILE], right[2 * Q_TILE:]
            w_right = jnp.exp2(-tot - (drop[:, SUB:] + right1)).astype(BF16)
            tot = tot + right1[:, 0:1] + keep[:, SUB:SUB + 1]
            w0 = jnp.exp2(-tot - (drop[:, :SUB] + right0))
            tot = tot + right0[:, 0:1] + keep[:, 0:1]
        weights = jnp.concatenate([w0.astype(BF16), w_right], axis=1)
        return tot, acc + _dot(weights, vb)

    carry = tile(i, (jnp.zeros((2 * Q_TILE, 1), F32), jnp.zeros((2 * Q_TILE, LANES), F32)), True)
    _, acc = lax.fori_loop(0, i, lambda t, cr: tile(i - 1 - t, cr, False), carry)
    o_ref[...] = jnp.where(first, acc[:Q_TILE], acc[Q_TILE:]).astype(BF16)


def _causal_mixer(body, name, q, k, v, qw, kw, vw, ow):
    b, s, _ = q.shape
    pairs = v.shape[2] // vw
    return pl.pallas_call(
        body,
        grid=(b, pairs, s // Q_TILE),
        in_specs=[pl.BlockSpec((None, Q_TILE, qw), lambda bi, g, i: (bi, i, g)),
                  pl.BlockSpec((None, s, kw), lambda bi, g, i: (bi, 0, g)),
                  pl.BlockSpec((None, s, vw), lambda bi, g, i: (bi, 0, g))],
        out_specs=pl.BlockSpec((None, Q_TILE, ow), lambda bi, g, i: (bi, i, g)),
        out_shape=jax.ShapeDtypeStruct((b, s, pairs * ow), BF16),
        compiler_params=pltpu.CompilerParams(dimension_semantics=("parallel", "parallel", "arbitrary"),
                                             vmem_limit_bytes=VMEM_LIMIT),
        name=name,
    )(q, k, v)


def _post_kernel(x_ref, oa_ref, ob_ref, om_ref, g_pre_ref, wg_ref, bg_ref, wbo_ref, wout_ref, g_post_ref,
                 g_mlp_ref, wup_ref, wdn_ref, g_mlp_post_ref, o_ref):
    x = x_ref[...]
    d = x.shape[1]
    h = _rms(x, g_pre_ref[...]).astype(BF16)
    merged = None
    for br, o_br in enumerate((oa_ref, ob_ref, om_ref)):
        sl = slice(br * d, (br + 1) * d)
        gate = jax.nn.sigmoid(_dot(h, wg_ref[:, sl]) + bg_ref[:, sl])
        yb = gate * _dot(o_br[...], wbo_ref[br])
        merged = yb if merged is None else merged + yb
    y = _dot(merged.astype(BF16), wout_ref[...])
    x1 = x + _rms(y, g_post_ref[...])

    h2 = _rms(x1, g_mlp_ref[...]).astype(BF16)
    down = None
    for c0 in range(0, wup_ref.shape[1], FF_CHUNK):
        u = jnp.maximum(_dot(h2, wup_ref[:, c0:c0 + FF_CHUNK]), 0.0)
        part = _dot((u * u).astype(BF16), wdn_ref[c0:c0 + FF_CHUNK, :])
        down = part if down is None else down + part
    o_ref[...] = x1 + _rms(down, g_mlp_post_ref[...])


def _post(x2d, oa, ob, om, g_pre, wg, bg, wbo, wout, g_post, g_mlp, wup, wdn, g_mlp_post):
    rows, d = x2d.shape
    tile = ROW_TILE
    row = lambda c: pl.BlockSpec((tile, c), lambda i: (i, 0))
    consts = (g_pre, wg, bg, wbo, wout, g_post, g_mlp, wup, wdn, g_mlp_post)
    return pl.pallas_call(
        _post_kernel,
        grid=(rows // tile,),
        in_specs=[row(d), row(oa.shape[1]), row(ob.shape[1]), row(om.shape[1])] + [_const_spec(a.shape) for a in consts],
        out_specs=row(d),
        out_shape=jax.ShapeDtypeStruct((rows, d), F32),
        compiler_params=pltpu.CompilerParams(dimension_semantics=("parallel",), vmem_limit_bytes=VMEM_LIMIT),
        name="post",
    )(x2d, oa, ob, om, *consts)


def _pad_heads(w, heads, width):
    r = w.shape[0]
    w = w.reshape(r, heads, width)
    return jnp.pad(w, ((0, 0), (0, 0), (0, LANES - width))).reshape(r, heads * LANES)


def _rotate_half_cols(w_rope):
    half = MLA_ROPE // 2
    return jnp.concatenate([-w_rope[..., half:], w_rope[..., :half]], axis=-1)


def _rope_slot(w_rope):
    r, heads, _ = w_rope.shape
    return jnp.pad(w_rope, ((0, 0), (0, 0), (MLA_NOPE, LANES - MLA_NOPE - MLA_ROPE))).reshape(r, heads * LANES)


def kernel(x, mem, positions, ln_mix_pre, w_in, b_gate, q_norm, w_uq, kv_norm, w_uk, w_uv, mem_norm, w_mem_kv,
           w_branch_out, w_out, ln_mix_post, ln_mlp_pre, w_mlp_up, w_mlp_down, ln_mlp_post):
    b, s, d = x.shape
    m = mem.shape[1]
    depth = w_in.shape[0]
    half = MLA_ROPE // 2
    inv_freq = 1.0 / (ROPE_THETA ** (jnp.arange(half, dtype=F32) * (2.0 / MLA_ROPE)))
    freq = jnp.zeros((1, LANES), F32).at[0, MLA_NOPE:MLA_NOPE + MLA_ROPE].set(jnp.concatenate([inv_freq, inv_freq]))
    ones = jnp.tile((jnp.arange(LANES) >= MLA_V).astype(F32), MLA_HEADS).reshape(1, -1)
    pos2d = positions.reshape(b * s, 1)
    mem2d = mem.reshape(b * m, d)
    x2d = x.reshape(b * s, d)
    row = lambda a: a.reshape(1, -1)

    for l in range(depth):
        cq_w, ckv_w, kr_w, sb_w, qm_w, gate_w = jnp.split(w_in[l], np.cumsum([_CQ, _CKV, MLA_ROPE, _SB, _QM]).tolist(), axis=1)
        kr3 = kr_w.reshape(d, 1, MLA_ROPE)
        w1 = jnp.concatenate([cq_w, ckv_w, sb_w, qm_w, _rope_slot(kr3), _rope_slot(_rotate_half_cols(kr3))], axis=1).astype(BF16)
        uq = w_uq[l].reshape(-1, MLA_HEADS, MLA_NOPE + MLA_ROPE)
        wuq = _pad_heads(w_uq[l], MLA_HEADS, MLA_NOPE + MLA_ROPE).astype(BF16)
        wuqr = _rope_slot(_rotate_half_cols(uq[..., MLA_NOPE:])).astype(BF16)
        wuk = _pad_heads(w_uk[l], MLA_HEADS, MLA_NOPE).astype(BF16)

        mkv = _mem_kv(mem2d, row(mem_norm[l]), w_mem_kv[l].astype(BF16))
        mkv = mkv.reshape(b, m, 2 * MEM_HEADS * MEM_DIM)

        q, k, v, sq, sk, sv, o_mem = _in_proj(
            x2d, pos2d, freq, row(ln_mix_pre[l]), w1, row(q_norm[l]), wuq, wuqr, row(kv_norm[l]), wuk,
            _pad_heads(w_uv[l], MLA_HEADS, MLA_V).astype(BF16), ones, mkv, s)

        seq = lambda a: a.reshape(b, s, -1)
        o_mla = _causal_mixer(_mla_kernel, "mla_attn", seq(q), seq(k), seq(v), 2 * LANES, 2 * LANES, 2 * LANES, LANES)
        o_sb = _causal_mixer(_sb_kernel, "sb_attn", seq(sq), seq(sk), seq(sv), LANES, LANES, LANES, LANES)

        x2d = _post(x2d, o_mla.reshape(b * s, -1), o_sb.reshape(b * s, -1), o_mem, row(ln_mix_pre[l]),
                    gate_w.astype(BF16), row(b_gate[l]), w_branch_out[l].astype(BF16), w_out[l].astype(BF16),
                    row(ln_mix_post[l]), row(ln_mlp_pre[l]), w_mlp_up[l].astype(BF16), w_mlp_down[l].astype(BF16),
                    row(ln_mlp_post[l]))
    return x2d.reshape(b, s, d)
```

```python
import functools

import jax
import jax.numpy as jnp
import numpy as np
from jax import lax
from jax.experimental import pallas as pl
from jax.experimental.pallas import tpu as pltpu

F32 = jnp.float32
BF16 = jnp.bfloat16

LANES = 128
EPS = 1e-6
ROPE_THETA = 10000.0

MLA_HEADS, MLA_NOPE, MLA_ROPE, MLA_V = 8, 64, 32, 64
SB_HEADS, SB_DIM = 8, 64
MEM_HEADS, MEM_DIM = 4, 128
N_BRANCH = 3
NEG_BIG = -1e30

ROW_TILE = 512
Q_TILE = 512
SB_TILE = 256
SB_DEAD_LOG2 = 160.0
LOG2E = 1.4426950408889634
FF_CHUNK = 1024
VMEM_LIMIT = 60 * 1024 * 1024


def _rms(x, g):
    return x * lax.rsqrt(jnp.mean(x * x, axis=-1, keepdims=True) + EPS) * g


def _dot(a, b):
    return jnp.dot(a, b, preferred_element_type=F32)


def _dot_t(a, b):
    return lax.dot_general(a, b, (((1,), (1,)), ((), ())), preferred_element_type=F32)


def _const_spec(shape):
    nd = len(shape)
    return pl.BlockSpec(shape, lambda *_: (0,) * nd, pipeline_mode=pl.Buffered(1))


def _mem_kv_kernel(mem_ref, g_ref, w_ref, o_ref):
    h = _rms(mem_ref[...], g_ref[...]).astype(BF16)
    o_ref[...] = _dot(h, w_ref[...]).astype(BF16)


def _mem_kv(mem2d, g, w):
    rows, d = mem2d.shape
    n = w.shape[1]
    tile = 512
    return pl.pallas_call(
        _mem_kv_kernel,
        grid=(rows // tile,),
        in_specs=[pl.BlockSpec((tile, d), lambda i: (i, 0)), _const_spec((1, d)), _const_spec((d, n))],
        out_specs=pl.BlockSpec((tile, n), lambda i: (i, 0)),
        out_shape=jax.ShapeDtypeStruct((rows, n), BF16),
        compiler_params=pltpu.CompilerParams(dimension_semantics=("parallel",), vmem_limit_bytes=VMEM_LIMIT),
        name="mem_kv",
    )(mem2d, g, w)


_CQ, _CKV, _SB, _QM, _KR = 384, 256, 3 * SB_HEADS * SB_DIM, MEM_HEADS * MEM_DIM, LANES
_O_CKV = _CQ
_O_SB = _O_CKV + _CKV
_O_QM = _O_SB + _SB
_O_KR = _O_QM + _QM
_O_KRR = _O_KR + _KR
_W1 = _O_KRR + _KR


def _in_proj_kernel(x_ref, pos_ref, freq_ref, g_ref, w1_ref, qn_ref, wuq_ref, wuqr_ref, kvn_ref, wuk_ref,
                    wuv_ref, ones_ref, mkv_ref, q_out, k_out, v_out, sq_out, sk_out, sv_out, om_out):
    h = _rms(x_ref[...], g_ref[...]).astype(BF16)
    proj = _dot(h, w1_ref[...])

    ang = pos_ref[...].astype(F32) * freq_ref[...]
    cos, sin = jnp.cos(ang), jnp.sin(ang)

    cq = _rms(proj[:, :_CQ], qn_ref[...]).astype(BF16)
    q = _dot(cq, wuq_ref[...])
    qr = _dot(cq, wuqr_ref[...])
    ckv = _rms(proj[:, _O_CKV:_O_SB], kvn_ref[...]).astype(BF16)
    kn = _dot(ckv, wuk_ref[...])
    v_out[...] = (_dot(ckv, wuv_ref[...]) + ones_ref[...]).astype(BF16)
    kr = proj[:, _O_KR:_O_KRR] * cos + proj[:, _O_KRR:_W1] * sin
    q_scale = (MLA_NOPE + MLA_ROPE) ** -0.5 * LOG2E
    for hd in range(MLA_HEADS):
        sl = slice(hd * LANES, (hd + 1) * LANES)
        q_out[:, sl] = ((q[:, sl] * cos + qr[:, sl] * sin) * q_scale).astype(BF16)
        k_out[:, sl] = (kn[:, sl] + kr).astype(BF16)

    w = SB_HEADS * SB_DIM
    sq_out[...] = (proj[:, _O_SB:_O_SB + w] * (SB_DIM ** -0.5 * LOG2E)).astype(BF16)
    sk_out[...] = proj[:, _O_SB + w:_O_SB + 2 * w].astype(BF16)
    sv_out[...] = proj[:, _O_SB + 2 * w:_O_QM].astype(BF16)

    for hd in range(MEM_HEADS):
        sl = slice(hd * MEM_DIM, (hd + 1) * MEM_DIM)
        vsl = slice(_QM + hd * MEM_DIM, _QM + (hd + 1) * MEM_DIM)
        qm = (proj[:, _O_QM + hd * MEM_DIM:_O_QM + (hd + 1) * MEM_DIM] * (MEM_DIM ** -0.5)).astype(BF16)
        s = _dot_t(qm, mkv_ref[:, sl])
        p = jnp.exp(s - jnp.max(s, axis=-1, keepdims=True))
        o = _dot(p.astype(BF16), mkv_ref[:, vsl])
        om_out[:, sl] = (o / jnp.sum(p, axis=-1, keepdims=True)).astype(BF16)


def _in_proj(x2d, pos2d, freq, g, w1, qn, wuq, wuqr, kvn, wuk, wuv, ones, mkv, seq):
    rows, d = x2d.shape
    tile = ROW_TILE
    per_seq = seq // tile
    mem_len = mkv.shape[1]
    row = lambda c: pl.BlockSpec((tile, c), lambda i: (i, 0))
    widths = (MLA_HEADS * LANES, MLA_HEADS * LANES, MLA_HEADS * LANES, SB_HEADS * SB_DIM, SB_HEADS * SB_DIM,
              SB_HEADS * SB_DIM, MEM_HEADS * MEM_DIM)
    mem_spec = pl.BlockSpec((None, mem_len, 2 * _QM), lambda i: (i // per_seq, 0, 0))
    return pl.pallas_call(
        _in_proj_kernel,
        grid=(rows // tile,),
        in_specs=[row(d), row(1), _const_spec(freq.shape), _const_spec(g.shape), _const_spec(w1.shape),
                  _const_spec(qn.shape), _const_spec(wuq.shape), _const_spec(wuqr.shape), _const_spec(kvn.shape),
                  _const_spec(wuk.shape), _const_spec(wuv.shape), _const_spec(ones.shape), mem_spec],
        out_specs=[row(c) for c in widths],
        out_shape=[jax.ShapeDtypeStruct((rows, c), BF16) for c in widths],
        compiler_params=pltpu.CompilerParams(dimension_semantics=("parallel",), vmem_limit_bytes=VMEM_LIMIT),
        name="in_proj",
    )(x2d, pos2d, freq, g, w1, qn, wuq, wuqr, kvn, wuk, wuv, ones, mkv)


def _key_rows(ref, j, tile):
    return ref[pl.ds(pl.multiple_of(j * tile, tile), tile), :]


def _cat_rows(parts):
    return jnp.concatenate(parts, axis=0)


def _lane_fold_max(x):
    top = x[:, :LANES]
    for c0 in range(LANES, x.shape[1], LANES):
        top = jnp.maximum(top, x[:, c0:c0 + LANES])
    return jnp.max(top, axis=-1, keepdims=True)


def _mla_kernel(q_ref, k_ref, v_ref, o_ref):
    i = pl.program_id(2)
    heads = [slice(hh * LANES, (hh + 1) * LANES) for hh in range(2)]

    def tile(j, carry, diag):
        ms, accs = carry
        kb = _key_rows(k_ref, j, Q_TILE)
        vb = _key_rows(v_ref, j, Q_TILE)
        scores = [_dot_t(q_ref[:, hs], kb[:, hs]) for hs in heads]
        if diag:
            r = lax.broadcasted_iota(jnp.int32, (Q_TILE, Q_TILE), 0)
            c = lax.broadcasted_iota(jnp.int32, (Q_TILE, Q_TILE), 1)
            scores = [jnp.where(r >= c, s, NEG_BIG) for s in scores]
        new_m, new_acc = [], []
        for hh, hs in enumerate(heads):
            m = jnp.maximum(ms[hh], _lane_fold_max(scores[hh]))
            p = jnp.exp2(scores[hh] - m).astype(BF16)
            new_m.append(m)
            new_acc.append(jnp.exp2(ms[hh] - m) * accs[hh] + _dot(p, vb[:, hs]))
        return tuple(new_m), tuple(new_acc)

    col = jnp.full((Q_TILE, 1), NEG_BIG, F32)
    zero = jnp.zeros((Q_TILE, LANES), F32)
    carry = lax.fori_loop(0, i, lambda j, cr: tile(j, cr, False), ((col, col), (zero, zero)))
    _, (acc0, acc1) = tile(i, carry, True)
    first = lax.broadcasted_iota(jnp.int32, (Q_TILE, LANES), 1) < MLA_V
    half_turn = lambda x: pltpu.roll(x, MLA_V, 1)
    o_ref[...] = jnp.where(first, acc0 / half_turn(acc0), half_turn(acc1) / acc1).astype(BF16)


def _sb_kernel(q_ref, k_ref, v_ref, o_ref):
    i = pl.program_id(2)
    first = lax.broadcasted_iota(jnp.int32, (SB_TILE, LANES), 1) < SB_DIM
    r = lax.broadcasted_iota(jnp.int32, (2 * SB_TILE, SB_TILE), 0) & (SB_TILE - 1)
    c = lax.broadcasted_iota(jnp.int32, (2 * SB_TILE, SB_TILE), 1)
    strict = r > c
    later = jnp.where(strict[:SB_TILE], 1.0, 0.0).astype(BF16)
    qp = q_ref[...]
    zq = jnp.zeros_like(qp)
    q2 = _cat_rows([jnp.where(first, qp, zq), jnp.where(first, zq, qp)])

    def tile(j, tot, acc, diag):
        kb = _key_rows(k_ref, j, SB_TILE)
        vb = _key_rows(v_ref, j, SB_TILE)
        z = _dot_t(q2, kb)
        lg = jnp.log2(1.0 + jnp.exp2(-jnp.abs(z)))
        drop = lg - jnp.minimum(z, 0.0)
        keep = z + drop
        if diag:
            keep = jnp.where(strict, keep, 0.0)
        right = _dot(keep.astype(BF16), later)
        w = jnp.exp2(-tot - (drop + right))
        if diag:
            w = jnp.where(strict, w, 0.0)
        return tot + right[:, 0:1] + keep[:, 0:1], acc + _dot(w.astype(BF16), vb)

    tot, acc = tile(i, jnp.zeros((2 * SB_TILE, 1), F32), jnp.zeros((2 * SB_TILE, LANES), F32), True)

    def live(state):
        t, go, _, _ = state
        return jnp.logical_and(t < i, go > 0)

    def step(state):
        t, _, tot, acc = state
        tot, acc = tile(i - 1 - t, tot, acc, False)
        return t + 1, (jnp.min(tot) < SB_DEAD_LOG2).astype(jnp.int32), tot, acc

    go = (jnp.min(tot) < SB_DEAD_LOG2).astype(jnp.int32)
    _, _, _, acc = lax.while_loop(live, step, (jnp.int32(0), go, tot, acc))
    o_ref[...] = jnp.where(first, acc[:SB_TILE], acc[SB_TILE:]).astype(BF16)


def _causal_mixer(body, name, tile, q, k, v, qw, kw, vw, ow):
    b, s, _ = q.shape
    pairs = v.shape[2] // vw
    return pl.pallas_call(
        body,
        grid=(b, pairs, s // tile),
        in_specs=[pl.BlockSpec((None, tile, qw), lambda bi, g, i: (bi, i, g)),
                  pl.BlockSpec((None, s, kw), lambda bi, g, i: (bi, 0, g)),
                  pl.BlockSpec((None, s, vw), lambda bi, g, i: (bi, 0, g))],
        out_specs=pl.BlockSpec((None, tile, ow), lambda bi, g, i: (bi, i, g)),
        out_shape=jax.ShapeDtypeStruct((b, s, pairs * ow), BF16),
        compiler_params=pltpu.CompilerParams(dimension_semantics=("parallel", "parallel", "arbitrary"),
                                             vmem_limit_bytes=VMEM_LIMIT),
        name=name,
    )(q, k, v)


def _post_kernel(x_ref, oa_ref, ob_ref, om_ref, g_pre_ref, wg_ref, bg_ref, wbo_ref, wout_ref, g_post_ref,
                 g_mlp_ref, wup_ref, wdn_ref, g_mlp_post_ref, o_ref):
    x = x_ref[...]
    d = x.shape[1]
    h = _rms(x, g_pre_ref[...]).astype(BF16)
    merged = None
    for br, o_br in enumerate((oa_ref, ob_ref, om_ref)):
        sl = slice(br * d, (br + 1) * d)
        gate = jax.nn.sigmoid(_dot(h, wg_ref[:, sl]) + bg_ref[:, sl])
        yb = gate * _dot(o_br[...], wbo_ref[br])
        merged = yb if merged is None else merged + yb
    y = _dot(merged.astype(BF16), wout_ref[...])
    x1 = x + _rms(y, g_post_ref[...])

    h2 = _rms(x1, g_mlp_ref[...]).astype(BF16)
    down = None
    for c0 in range(0, wup_ref.shape[1], FF_CHUNK):
        u = jnp.maximum(_dot(h2, wup_ref[:, c0:c0 + FF_CHUNK]), 0.0)
        part = _dot((u * u).astype(BF16), wdn_ref[c0:c0 + FF_CHUNK, :])
        down = part if down is None else down + part
    o_ref[...] = x1 + _rms(down, g_mlp_post_ref[...])


def _post(x2d, oa, ob, om, g_pre, wg, bg, wbo, wout, g_post, g_mlp, wup, wdn, g_mlp_post):
    rows, d = x2d.shape
    tile = ROW_TILE
    row = lambda c: pl.BlockSpec((tile, c), lambda i: (i, 0))
    consts = (g_pre, wg, bg, wbo, wout, g_post, g_mlp, wup, wdn, g_mlp_post)
    return pl.pallas_call(
        _post_kernel,
        grid=(rows // tile,),
        in_specs=[row(d), row(oa.shape[1]), row(ob.shape[1]), row(om.shape[1])] + [_const_spec(a.shape) for a in consts],
        out_specs=row(d),
        out_shape=jax.ShapeDtypeStruct((rows, d), F32),
        compiler_params=pltpu.CompilerParams(dimension_semantics=("parallel",), vmem_limit_bytes=VMEM_LIMIT),
        name="post",
    )(x2d, oa, ob, om, *consts)


def _pad_heads(w, heads, width):
    r = w.shape[0]
    w = w.reshape(r, heads, width)
    return jnp.pad(w, ((0, 0), (0, 0), (0, LANES - width))).reshape(r, heads * LANES)


def _rotate_half_cols(w_rope):
    half = MLA_ROPE // 2
    return jnp.concatenate([-w_rope[..., half:], w_rope[..., :half]], axis=-1)


def _rope_slot(w_rope):
    r, heads, _ = w_rope.shape
    return jnp.pad(w_rope, ((0, 0), (0, 0), (MLA_NOPE, LANES - MLA_NOPE - MLA_ROPE))).reshape(r, heads * LANES)


def kernel(x, mem, positions, ln_mix_pre, w_in, b_gate, q_norm, w_uq, kv_norm, w_uk, w_uv, mem_norm, w_mem_kv,
           w_branch_out, w_out, ln_mix_post, ln_mlp_pre, w_mlp_up, w_mlp_down, ln_mlp_post):
    b, s, d = x.shape
    m = mem.shape[1]
    depth = w_in.shape[0]
    half = MLA_ROPE // 2
    inv_freq = 1.0 / (ROPE_THETA ** (jnp.arange(half, dtype=F32) * (2.0 / MLA_ROPE)))
    freq = jnp.zeros((1, LANES), F32).at[0, MLA_NOPE:MLA_NOPE + MLA_ROPE].set(jnp.concatenate([inv_freq, inv_freq]))
    ones = jnp.tile((jnp.arange(LANES) >= MLA_V).astype(F32), MLA_HEADS).reshape(1, -1)
    pos2d = positions.reshape(b * s, 1)
    mem2d = mem.reshape(b * m, d)
    x2d = x.reshape(b * s, d)
    row = lambda a: a.reshape(1, -1)

    for l in range(depth):
        cq_w, ckv_w, kr_w, sb_w, qm_w, gate_w = jnp.split(w_in[l], np.cumsum([_CQ, _CKV, MLA_ROPE, _SB, _QM]).tolist(), axis=1)
        kr3 = kr_w.reshape(d, 1, MLA_ROPE)
        w1 = jnp.concatenate([cq_w, ckv_w, sb_w, qm_w, _rope_slot(kr3), _rope_slot(_rotate_half_cols(kr3))], axis=1).astype(BF16)
        uq = w_uq[l].reshape(-1, MLA_HEADS, MLA_NOPE + MLA_ROPE)
        wuq = _pad_heads(w_uq[l], MLA_HEADS, MLA_NOPE + MLA_ROPE).astype(BF16)
        wuqr = _rope_slot(_rotate_half_cols(uq[..., MLA_NOPE:])).astype(BF16)
        wuk = _pad_heads(w_uk[l], MLA_HEADS, MLA_NOPE).astype(BF16)

        mkv = _mem_kv(mem2d, row(mem_norm[l]), w_mem_kv[l].astype(BF16))
        mkv = mkv.reshape(b, m, 2 * MEM_HEADS * MEM_DIM)

        q, k, v, sq, sk, sv, o_mem = _in_proj(
            x2d, pos2d, freq, row(ln_mix_pre[l]), w1, row(q_norm[l]), wuq, wuqr, row(kv_norm[l]), wuk,
            _pad_heads(w_uv[l], MLA_HEADS, MLA_V).astype(BF16), ones, mkv, s)

        seq = lambda a: a.reshape(b, s, -1)
        o_mla = _causal_mixer(_mla_kernel, "mla_attn", Q_TILE, seq(q), seq(k), seq(v), 2 * LANES, 2 * LANES, 2 * LANES, LANES)
        o_sb = _causal_mixer(_sb_kernel, "sb_attn", SB_TILE, seq(sq), seq(sk), seq(sv), LANES, LANES, LANES, LANES)

        x2d = _post(x2d, o_mla.reshape(b * s, -1), o_sb.reshape(b * s, -1), o_mem, row(ln_mix_pre[l]),
                    gate_w.astype(BF16), row(b_gate[l]), w_branch_out[l].astype(BF16), w_out[l].astype(BF16),
                    row(ln_mix_post[l]), row(ln_mlp_pre[l]), w_mlp_up[l].astype(BF16), w_mlp_down[l].astype(BF16),
                    row(ln_mlp_post[l]))
    return x2d.reshape(b, s, d)
```

```python
import functools

import jax
import jax.numpy as jnp
import numpy as np
from jax import lax
from jax.experimental import pallas as pl
from jax.experimental.pallas import tpu as pltpu

F32 = jnp.float32
BF16 = jnp.bfloat16

LANES = 128
EPS = 1e-6
ROPE_THETA = 10000.0

MLA_HEADS, MLA_NOPE, MLA_ROPE, MLA_V = 8, 64, 32, 64
SB_HEADS, SB_DIM = 8, 64
MEM_HEADS, MEM_DIM = 4, 128
N_BRANCH = 3
NEG_BIG = -1e30

ROW_TILE = 512
Q_TILE = 512
SB_TILE = 256
MLA_PAIRS, SB_PAIRS = 2, 4
SB_DEAD_LOG2 = 160.0
LOG2E = 1.4426950408889634
FF_CHUNK = 1024
VMEM_LIMIT = 60 * 1024 * 1024


def _rms(x, g):
    return x * lax.rsqrt(jnp.mean(x * x, axis=-1, keepdims=True) + EPS) * g


def _dot(a, b):
    return jnp.dot(a, b, preferred_element_type=F32)


def _dot_t(a, b):
    return lax.dot_general(a, b, (((1,), (1,)), ((), ())), preferred_element_type=F32)


def _const_spec(shape):
    nd = len(shape)
    return pl.BlockSpec(shape, lambda *_: (0,) * nd, pipeline_mode=pl.Buffered(1))


def _mem_kv_kernel(mem_ref, g_ref, w_ref, o_ref):
    h = _rms(mem_ref[...], g_ref[...]).astype(BF16)
    o_ref[...] = _dot(h, w_ref[...]).astype(BF16)


def _mem_kv(mem2d, g, w):
    rows, d = mem2d.shape
    n = w.shape[1]
    tile = 512
    return pl.pallas_call(
        _mem_kv_kernel,
        grid=(rows // tile,),
        in_specs=[pl.BlockSpec((tile, d), lambda i: (i, 0)), _const_spec((1, d)), _const_spec((d, n))],
        out_specs=pl.BlockSpec((tile, n), lambda i: (i, 0)),
        out_shape=jax.ShapeDtypeStruct((rows, n), BF16),
        compiler_params=pltpu.CompilerParams(dimension_semantics=("parallel",), vmem_limit_bytes=VMEM_LIMIT),
        name="mem_kv",
    )(mem2d, g, w)


_CQ, _CKV, _SB, _QM, _KR = 384, 256, 3 * SB_HEADS * SB_DIM, MEM_HEADS * MEM_DIM, LANES
_O_CKV = _CQ
_O_SB = _O_CKV + _CKV
_O_QM = _O_SB + _SB
_O_KR = _O_QM + _QM
_O_KRR = _O_KR + _KR
_W1 = _O_KRR + _KR


def _in_proj_kernel(x_ref, pos_ref, freq_ref, g_ref, w1_ref, qn_ref, wuq_ref, wuqr_ref, kvn_ref, wuk_ref,
                    wuv_ref, ones_ref, mkv_ref, q_out, k_out, v_out, sq_out, sk_out, sv_out, om_out):
    h = _rms(x_ref[...], g_ref[...]).astype(BF16)
    proj = _dot(h, w1_ref[...])

    ang = pos_ref[...].astype(F32) * freq_ref[...]
    cos, sin = jnp.cos(ang), jnp.sin(ang)

    cq = _rms(proj[:, :_CQ], qn_ref[...]).astype(BF16)
    q = _dot(cq, wuq_ref[...])
    qr = _dot(cq, wuqr_ref[...])
    ckv = _rms(proj[:, _O_CKV:_O_SB], kvn_ref[...]).astype(BF16)
    kn = _dot(ckv, wuk_ref[...])
    v_out[...] = (_dot(ckv, wuv_ref[...]) + ones_ref[...]).astype(BF16)
    kr = proj[:, _O_KR:_O_KRR] * cos + proj[:, _O_KRR:_W1] * sin
    q_scale = (MLA_NOPE + MLA_ROPE) ** -0.5 * LOG2E
    for hd in range(MLA_HEADS):
        sl = slice(hd * LANES, (hd + 1) * LANES)
        q_out[:, sl] = ((q[:, sl] * cos + qr[:, sl] * sin) * q_scale).astype(BF16)
        k_out[:, sl] = (kn[:, sl] + kr).astype(BF16)

    w = SB_HEADS * SB_DIM
    sq_out[...] = (proj[:, _O_SB:_O_SB + w] * (SB_DIM ** -0.5 * LOG2E)).astype(BF16)
    sk_out[...] = proj[:, _O_SB + w:_O_SB + 2 * w].astype(BF16)
    sv_out[...] = proj[:, _O_SB + 2 * w:_O_QM].astype(BF16)

    for hd in range(MEM_HEADS):
        sl = slice(hd * MEM_DIM, (hd + 1) * MEM_DIM)
        vsl = slice(_QM + hd * MEM_DIM, _QM + (hd + 1) * MEM_DIM)
        qm = (proj[:, _O_QM + hd * MEM_DIM:_O_QM + (hd + 1) * MEM_DIM] * (MEM_DIM ** -0.5)).astype(BF16)
        s = _dot_t(qm, mkv_ref[:, sl])
        p = jnp.exp(s - jnp.max(s, axis=-1, keepdims=True))
        o = _dot(p.astype(BF16), mkv_ref[:, vsl])
        om_out[:, sl] = (o / jnp.sum(p, axis=-1, keepdims=True)).astype(BF16)


def _in_proj(x2d, pos2d, freq, g, w1, qn, wuq, wuqr, kvn, wuk, wuv, ones, mkv, seq):
    rows, d = x2d.shape
    tile = ROW_TILE
    per_seq = seq // tile
    mem_len = mkv.shape[1]
    row = lambda c: pl.BlockSpec((tile, c), lambda i: (i, 0))
    widths = (MLA_HEADS * LANES, MLA_HEADS * LANES, MLA_HEADS * LANES, SB_HEADS * SB_DIM, SB_HEADS * SB_DIM,
              SB_HEADS * SB_DIM, MEM_HEADS * MEM_DIM)
    mem_spec = pl.BlockSpec((None, mem_len, 2 * _QM), lambda i: (i // per_seq, 0, 0))
    return pl.pallas_call(
        _in_proj_kernel,
        grid=(rows // tile,),
        in_specs=[row(d), row(1), _const_spec(freq.shape), _const_spec(g.shape), _const_spec(w1.shape),
                  _const_spec(qn.shape), _const_spec(wuq.shape), _const_spec(wuqr.shape), _const_spec(kvn.shape),
                  _const_spec(wuk.shape), _const_spec(wuv.shape), _const_spec(ones.shape), mem_spec],
        out_specs=[row(c) for c in widths],
        out_shape=[jax.ShapeDtypeStruct((rows, c), BF16) for c in widths],
        compiler_params=pltpu.CompilerParams(dimension_semantics=("parallel",), vmem_limit_bytes=VMEM_LIMIT),
        name="in_proj",
    )(x2d, pos2d, freq, g, w1, qn, wuq, wuqr, kvn, wuk, wuv, ones, mkv)


def _key_rows(ref, j, tile):
    return ref[pl.ds(pl.multiple_of(j * tile, tile), tile), :]


def _cat_rows(parts):
    return jnp.concatenate(parts, axis=0)


def _lane_fold_max(x):
    top = x[:, :LANES]
    for c0 in range(LANES, x.shape[1], LANES):
        top = jnp.maximum(top, x[:, c0:c0 + LANES])
    return jnp.max(top, axis=-1, keepdims=True)


def _mla_kernel(q_ref, k_ref, v_ref, o_ref):
    i = pl.program_id(2)
    heads = [slice(hh * LANES, (hh + 1) * LANES) for hh in range(q_ref.shape[1] // LANES)]

    def tile(j, carry, diag):
        ms, accs = carry
        kb = _key_rows(k_ref, j, Q_TILE)
        vb = _key_rows(v_ref, j, Q_TILE)
        scores = [_dot_t(q_ref[:, hs], kb[:, hs]) for hs in heads]
        if diag:
            r = lax.broadcasted_iota(jnp.int32, (Q_TILE, Q_TILE), 0)
            c = lax.broadcasted_iota(jnp.int32, (Q_TILE, Q_TILE), 1)
            scores = [jnp.where(r >= c, s, NEG_BIG) for s in scores]
        new_m, new_acc = [], []
        for hh, hs in enumerate(heads):
            m = jnp.maximum(ms[hh], _lane_fold_max(scores[hh]))
            p = jnp.exp2(scores[hh] - m).astype(BF16)
            new_m.append(m)
            new_acc.append(jnp.exp2(ms[hh] - m) * accs[hh] + _dot(p, vb[:, hs]))
        return tuple(new_m), tuple(new_acc)

    col = jnp.full((Q_TILE, 1), NEG_BIG, F32)
    zero = jnp.zeros((Q_TILE, LANES), F32)
    init = ((col,) * len(heads), (zero,) * len(heads))
    carry = lax.fori_loop(0, i, lambda j, cr: tile(j, cr, False), init)
    _, accs = tile(i, carry, True)
    first = lax.broadcasted_iota(jnp.int32, (Q_TILE, LANES), 1) < MLA_V
    half_turn = lambda x: pltpu.roll(x, MLA_V, 1)
    for pr in range(len(heads) // 2):
        even, odd = accs[2 * pr], accs[2 * pr + 1]
        o_ref[:, heads[pr]] = jnp.where(first, even / half_turn(even), half_turn(odd) / odd).astype(BF16)


def _sb_kernel(q_ref, k_ref, v_ref, o_ref):
    i = pl.program_id(2)
    slots = [slice(sl * LANES, (sl + 1) * LANES) for sl in range(q_ref.shape[1] // LANES)]
    first = lax.broadcasted_iota(jnp.int32, (SB_TILE, LANES), 1) < SB_DIM
    r = lax.broadcasted_iota(jnp.int32, (2 * SB_TILE, SB_TILE), 0) & (SB_TILE - 1)
    c = lax.broadcasted_iota(jnp.int32, (2 * SB_TILE, SB_TILE), 1)
    strict = r > c
    later = jnp.where(strict[:SB_TILE], 1.0, 0.0).astype(BF16)
    q2 = []
    for sl in slots:
        qp = q_ref[:, sl]
        zq = jnp.zeros_like(qp)
        q2.append(_cat_rows([jnp.where(first, qp, zq), jnp.where(first, zq, qp)]))

    def tile(j, tots, accs, diag):
        kb = _key_rows(k_ref, j, SB_TILE)
        vb = _key_rows(v_ref, j, SB_TILE)
        logits = [_dot_t(q2[n], kb[:, sl]) for n, sl in enumerate(slots)]
        drops, keeps = [], []
        for z in logits:
            lg = jnp.log2(1.0 + jnp.exp2(-jnp.abs(z)))
            drop = lg - jnp.minimum(z, 0.0)
            keep = z + drop
            drops.append(drop)
            keeps.append(jnp.where(strict, keep, 0.0) if diag else keep)
        rights = [_dot(keep.astype(BF16), later) for keep in keeps]
        new_tots, new_accs = [], []
        for n, sl in enumerate(slots):
            w = jnp.exp2(-tots[n] - (drops[n] + rights[n]))
            if diag:
                w = jnp.where(strict, w, 0.0)
            new_tots.append(tots[n] + rights[n][:, 0:1] + keeps[n][:, 0:1])
            new_accs.append(accs[n] + _dot(w.astype(BF16), vb[:, sl]))
        return tuple(new_tots), tuple(new_accs)

    def any_live(tots):
        low = tots[0]
        for tot in tots[1:]:
            low = jnp.minimum(low, tot)
        return (jnp.min(low) < SB_DEAD_LOG2).astype(jnp.int32)

    zero_tot = jnp.zeros((2 * SB_TILE, 1), F32)
    zero_acc = jnp.zeros((2 * SB_TILE, LANES), F32)
    tots, accs = tile(i, (zero_tot,) * len(slots), (zero_acc,) * len(slots), True)

    def live(state):
        t, go, _, _ = state
        return jnp.logical_and(t < i, go > 0)

    def step(state):
        t, _, tots, accs = state
        tots, accs = tile(i - 1 - t, tots, accs, False)
        return t + 1, any_live(tots), tots, accs

    _, _, _, accs = lax.while_loop(live, step, (jnp.int32(0), any_live(tots), tots, accs))
    for n, sl in enumerate(slots):
        o_ref[:, sl] = jnp.where(first, accs[n][:SB_TILE], accs[n][SB_TILE:]).astype(BF16)


def _causal_mixer(body, name, tile, pairs, q, k, v, qw, kw, vw, ow):
    b, s, _ = q.shape
    groups = v.shape[2] // (vw * pairs)
    qw, kw, vw, ow = qw * pairs, kw * pairs, vw * pairs, ow * pairs
    return pl.pallas_call(
        body,
        grid=(b, groups, s // tile),
        in_specs=[pl.BlockSpec((None, tile, qw), lambda bi, g, i: (bi, i, g)),
                  pl.BlockSpec((None, s, kw), lambda bi, g, i: (bi, 0, g)),
                  pl.BlockSpec((None, s, vw), lambda bi, g, i: (bi, 0, g))],
        out_specs=pl.BlockSpec((None, tile, ow), lambda bi, g, i: (bi, i, g)),
        out_shape=jax.ShapeDtypeStruct((b, s, groups * ow), BF16),
        compiler_params=pltpu.CompilerParams(dimension_semantics=("parallel", "parallel", "arbitrary"),
                                             vmem_limit_bytes=VMEM_LIMIT),
        name=name,
    )(q, k, v)


def _post_kernel(x_ref, oa_ref, ob_ref, om_ref, g_pre_ref, wg_ref, bg_ref, wbo_ref, wout_ref, g_post_ref,
                 g_mlp_ref, wup_ref, wdn_ref, g_mlp_post_ref, o_ref):
    x = x_ref[...]
    d = x.shape[1]
    h = _rms(x, g_pre_ref[...]).astype(BF16)
    merged = None
    for br, o_br in enumerate((oa_ref, ob_ref, om_ref)):
        sl = slice(br * d, (br + 1) * d)
        gate = jax.nn.sigmoid(_dot(h, wg_ref[:, sl]) + bg_ref[:, sl])
        yb = gate * _dot(o_br[...], wbo_ref[br])
        merged = yb if merged is None else merged + yb
    y = _dot(merged.astype(BF16), wout_ref[...])
    x1 = x + _rms(y, g_post_ref[...])

    h2 = _rms(x1, g_mlp_ref[...]).astype(BF16)
    down = None
    for c0 in range(0, wup_ref.shape[1], FF_CHUNK):
        u = jnp.maximum(_dot(h2, wup_ref[:, c0:c0 + FF_CHUNK]), 0.0)
        part = _dot((u * u).astype(BF16), wdn_ref[c0:c0 + FF_CHUNK, :])
        down = part if down is None else down + part
    o_ref[...] = x1 + _rms(down, g_mlp_post_ref[...])


def _post(x2d, oa, ob, om, g_pre, wg, bg, wbo, wout, g_post, g_mlp, wup, wdn, g_mlp_post):
    rows, d = x2d.shape
    tile = ROW_TILE
    row = lambda c: pl.BlockSpec((tile, c), lambda i: (i, 0))
    consts = (g_pre, wg, bg, wbo, wout, g_post, g_mlp, wup, wdn, g_mlp_post)
    return pl.pallas_call(
        _post_kernel,
        grid=(rows // tile,),
        in_specs=[row(d), row(oa.shape[1]), row(ob.shape[1]), row(om.shape[1])] + [_const_spec(a.shape) for a in consts],
        out_specs=row(d),
        out_shape=jax.ShapeDtypeStruct((rows, d), F32),
        compiler_params=pltpu.CompilerParams(dimension_semantics=("parallel",), vmem_limit_bytes=VMEM_LIMIT),
        name="post",
    )(x2d, oa, ob, om, *consts)


def _pad_heads(w, heads, width):
    r = w.shape[0]
    w = w.reshape(r, heads, width)
    return jnp.pad(w, ((0, 0), (0, 0), (0, LANES - width))).reshape(r, heads * LANES)


def _rotate_half_cols(w_rope):
    half = MLA_ROPE // 2
    return jnp.concatenate([-w_rope[..., half:], w_rope[..., :half]], axis=-1)


def _rope_slot(w_rope):
    r, heads, _ = w_rope.shape
    return jnp.pad(w_rope, ((0, 0), (0, 0), (MLA_NOPE, LANES - MLA_NOPE - MLA_ROPE))).reshape(r, heads * LANES)


def kernel(x, mem, positions, ln_mix_pre, w_in, b_gate, q_norm, w_uq, kv_norm, w_uk, w_uv, mem_norm, w_mem_kv,
           w_branch_out, w_out, ln_mix_post, ln_mlp_pre, w_mlp_up, w_mlp_down, ln_mlp_post):
    b, s, d = x.shape
    m = mem.shape[1]
    depth = w_in.shape[0]
    half = MLA_ROPE // 2
    inv_freq = 1.0 / (ROPE_THETA ** (jnp.arange(half, dtype=F32) * (2.0 / MLA_ROPE)))
    freq = jnp.zeros((1, LANES), F32).at[0, MLA_NOPE:MLA_NOPE + MLA_ROPE].set(jnp.concatenate([inv_freq, inv_freq]))
    ones = jnp.tile((jnp.arange(LANES) >= MLA_V).astype(F32), MLA_HEADS).reshape(1, -1)
    pos2d = positions.reshape(b * s, 1)
    mem2d = mem.reshape(b * m, d)
    x2d = x.reshape(b * s, d)
    row = lambda a: a.reshape(1, -1)

    for l in range(depth):
        cq_w, ckv_w, kr_w, sb_w, qm_w, gate_w = jnp.split(w_in[l], np.cumsum([_CQ, _CKV, MLA_ROPE, _SB, _QM]).tolist(), axis=1)
        kr3 = kr_w.reshape(d, 1, MLA_ROPE)
        w1 = jnp.concatenate([cq_w, ckv_w, sb_w, qm_w, _rope_slot(kr3), _rope_slot(_rotate_half_cols(kr3))], axis=1).astype(BF16)
        uq = w_uq[l].reshape(-1, MLA_HEADS, MLA_NOPE + MLA_ROPE)
        wuq = _pad_heads(w_uq[l], MLA_HEADS, MLA_NOPE + MLA_ROPE).astype(BF16)
        wuqr = _rope_slot(_rotate_half_cols(uq[..., MLA_NOPE:])).astype(BF16)
        wuk = _pad_heads(w_uk[l], MLA_HEADS, MLA_NOPE).astype(BF16)

        mkv = _mem_kv(mem2d, row(mem_norm[l]), w_mem_kv[l].astype(BF16))
        mkv = mkv.reshape(b, m, 2 * MEM_HEADS * MEM_DIM)

        q, k, v, sq, sk, sv, o_mem = _in_proj(
            x2d, pos2d, freq, row(ln_mix_pre[l]), w1, row(q_norm[l]), wuq, wuqr, row(kv_norm[l]), wuk,
            _pad_heads(w_uv[l], MLA_HEADS, MLA_V).astype(BF16), ones, mkv, s)

        seq = lambda a: a.reshape(b, s, -1)
        o_mla = _causal_mixer(_mla_kernel, "mla_attn", Q_TILE, MLA_PAIRS, seq(q), seq(k), seq(v),
                              2 * LANES, 2 * LANES, 2 * LANES, LANES)
        o_sb = _causal_mixer(_sb_kernel, "sb_attn", SB_TILE, SB_PAIRS, seq(sq), seq(sk), seq(sv),
                             LANES, LANES, LANES, LANES)

        x2d = _post(x2d, o_mla.reshape(b * s, -1), o_sb.reshape(b * s, -1), o_mem, row(ln_mix_pre[l]),
                    gate_w.astype(BF16), row(b_gate[l]), w_branch_out[l].astype(BF16), w_out[l].astype(BF16),
                    row(ln_mix_post[l]), row(ln_mlp_pre[l]), w_mlp_up[l].astype(BF16), w_mlp_down[l].astype(BF16),
                    row(ln_mlp_post[l]))
    return x2d.reshape(b, s, d)
```

```python
import functools

import jax
import jax.numpy as jnp
import numpy as np
from jax import lax
from jax.experimental import pallas as pl
from jax.experimental.pallas import tpu as pltpu

F32 = jnp.float32
BF16 = jnp.bfloat16

LANES = 128
EPS = 1e-6
ROPE_THETA = 10000.0

MLA_HEADS, MLA_NOPE, MLA_ROPE, MLA_V = 8, 64, 32, 64
SB_HEADS, SB_DIM = 8, 64
MEM_HEADS, MEM_DIM = 4, 128
N_BRANCH = 3
NEG_BIG = -1e30

IN_TILE, POST_TILE = 1024, 512
Q_TILE = 512
SB_TILE = 256
MLA_PAIRS, SB_PAIRS = 2, 4
SB_DEAD_LOG2 = 160.0
LOG2E = 1.4426950408889634
FF_CHUNK = 1024
VMEM_LIMIT = 60 * 1024 * 1024


def _rms(x, g):
    return x * lax.rsqrt(jnp.mean(x * x, axis=-1, keepdims=True) + EPS) * g


def _dot(a, b):
    return jnp.dot(a, b, preferred_element_type=F32)


def _dot_t(a, b):
    return lax.dot_general(a, b, (((1,), (1,)), ((), ())), preferred_element_type=F32)


def _const_spec(shape):
    nd = len(shape)
    return pl.BlockSpec(shape, lambda *_: (0,) * nd, pipeline_mode=pl.Buffered(1))


def _mem_kv_kernel(mem_ref, g_ref, w_ref, o_ref):
    h = _rms(mem_ref[...], g_ref[...]).astype(BF16)
    o_ref[...] = _dot(h, w_ref[...]).astype(BF16)


def _mem_kv(mem2d, g, w):
    rows, d = mem2d.shape
    n = w.shape[1]
    tile = 512
    return pl.pallas_call(
        _mem_kv_kernel,
        grid=(rows // tile,),
        in_specs=[pl.BlockSpec((tile, d), lambda i: (i, 0)), _const_spec((1, d)), _const_spec((d, n))],
        out_specs=pl.BlockSpec((tile, n), lambda i: (i, 0)),
        out_shape=jax.ShapeDtypeStruct((rows, n), BF16),
        compiler_params=pltpu.CompilerParams(dimension_semantics=("parallel",), vmem_limit_bytes=VMEM_LIMIT),
        name="mem_kv",
    )(mem2d, g, w)


_CQ, _CKV, _SB, _QM, _KR = 384, 256, 3 * SB_HEADS * SB_DIM, MEM_HEADS * MEM_DIM, LANES
_O_CKV = _CQ
_O_SB = _O_CKV + _CKV
_O_QM = _O_SB + _SB
_O_KR = _O_QM + _QM
_O_KRR = _O_KR + _KR
_W1 = _O_KRR + _KR
MXU_WIDTH = 256
_W1_PAD = -_W1 % MXU_WIDTH


def _in_proj_kernel(x_ref, pos_ref, freq_ref, g_ref, w1_ref, qn_ref, wuq_ref, wuqr_ref, kvn_ref, wuk_ref,
                    wuv_ref, ones_ref, mkv_ref, q_out, k_out, v_out, sq_out, sk_out, sv_out, om_out):
    groups = LANES // MLA_ROPE
    blk = pos_ref.shape[0] // groups
    pos = pos_ref[...].astype(F32)
    ang = pos[:blk] * freq_ref[0:1, :]
    for gi in range(1, groups):
        ang = ang + pos[gi * blk:(gi + 1) * blk] * freq_ref[gi:gi + 1, :]
    cos_packed, sin_packed = jnp.cos(ang), jnp.sin(ang)
    lane = lax.broadcasted_iota(jnp.int32, (blk, LANES), 1)
    rotary = jnp.logical_and(lane >= MLA_NOPE, lane < MLA_NOPE + MLA_ROPE)
    cos_blocks, sin_blocks = [], []
    for gi in range(groups):
        shift = (MLA_NOPE - gi * MLA_ROPE) % LANES
        place = (lambda t: t) if shift == 0 else (lambda t: pltpu.roll(t, shift, 1))
        cos_blocks.append(jnp.where(rotary, place(cos_packed), 1.0))
        sin_blocks.append(jnp.where(rotary, place(sin_packed), 0.0))
    cos, sin = jnp.concatenate(cos_blocks, axis=0), jnp.concatenate(sin_blocks, axis=0)

    h = _rms(x_ref[...], g_ref[...]).astype(BF16)
    proj = _dot(h, w1_ref[...])

    cq = _rms(proj[:, :_CQ], qn_ref[...]).astype(BF16)
    q = _dot(cq, wuq_ref[...])
    qr = _dot(cq, wuqr_ref[...])
    ckv = _rms(proj[:, _O_CKV:_O_SB], kvn_ref[...]).astype(BF16)
    kn = _dot(ckv, wuk_ref[...])
    v_out[...] = (_dot(ckv, wuv_ref[...]) + ones_ref[...]).astype(BF16)
    kr = proj[:, _O_KR:_O_KRR] * cos + proj[:, _O_KRR:_W1] * sin
    q_scale = (MLA_NOPE + MLA_ROPE) ** -0.5 * LOG2E
    for hd in range(MLA_HEADS):
        sl = slice(hd * LANES, (hd + 1) * LANES)
        q_out[:, sl] = ((q[:, sl] * cos + qr[:, sl] * sin) * q_scale).astype(BF16)
        k_out[:, sl] = (kn[:, sl] + kr).astype(BF16)

    w = SB_HEADS * SB_DIM
    sq_out[...] = (proj[:, _O_SB:_O_SB + w] * (SB_DIM ** -0.5 * LOG2E)).astype(BF16)
    sk_out[...] = proj[:, _O_SB + w:_O_SB + 2 * w].astype(BF16)
    sv_out[...] = proj[:, _O_SB + 2 * w:_O_QM].astype(BF16)

    for hd in range(MEM_HEADS):
        sl = slice(hd * MEM_DIM, (hd + 1) * MEM_DIM)
        vsl = slice(_QM + hd * MEM_DIM, _QM + (hd + 1) * MEM_DIM)
        qm = (proj[:, _O_QM + hd * MEM_DIM:_O_QM + (hd + 1) * MEM_DIM] * (MEM_DIM ** -0.5)).astype(BF16)
        s = _dot_t(qm, mkv_ref[:, sl])
        p = jnp.exp(s - jnp.max(s, axis=-1, keepdims=True))
        o = _dot(p.astype(BF16), mkv_ref[:, vsl])
        om_out[:, sl] = (o / jnp.sum(p, axis=-1, keepdims=True)).astype(BF16)


def _in_proj(x2d, pos2d, freq, g, w1, qn, wuq, wuqr, kvn, wuk, wuv, ones, mkv, seq):
    rows, d = x2d.shape
    tile = IN_TILE
    per_seq = seq // tile
    mem_len = mkv.shape[1]
    row = lambda c: pl.BlockSpec((tile, c), lambda i: (i, 0))
    widths = (MLA_HEADS * LANES, MLA_HEADS * LANES, MLA_HEADS * LANES, SB_HEADS * SB_DIM, SB_HEADS * SB_DIM,
              SB_HEADS * SB_DIM, MEM_HEADS * MEM_DIM)
    mem_spec = pl.BlockSpec((None, mem_len, 2 * _QM), lambda i: (i // per_seq, 0, 0))
    return pl.pallas_call(
        _in_proj_kernel,
        grid=(rows // tile,),
        in_specs=[row(d), row(1), _const_spec(freq.shape), _const_spec(g.shape), _const_spec(w1.shape),
                  _const_spec(qn.shape), _const_spec(wuq.shape), _const_spec(wuqr.shape), _const_spec(kvn.shape),
                  _const_spec(wuk.shape), _const_spec(wuv.shape), _const_spec(ones.shape), mem_spec],
        out_specs=[row(c) for c in widths],
        out_shape=[jax.ShapeDtypeStruct((rows, c), BF16) for c in widths],
        compiler_params=pltpu.CompilerParams(dimension_semantics=("parallel",), vmem_limit_bytes=VMEM_LIMIT),
        name="in_proj",
    )(x2d, pos2d, freq, g, w1, qn, wuq, wuqr, kvn, wuk, wuv, ones, mkv)


def _key_rows(ref, j, tile):
    return ref[pl.ds(pl.multiple_of(j * tile, tile), tile), :]


def _cat_rows(parts):
    return jnp.concatenate(parts, axis=0)


def _lane_fold_max(x):
    top = x[:, :LANES]
    for c0 in range(LANES, x.shape[1], LANES):
        top = jnp.maximum(top, x[:, c0:c0 + LANES])
    return jnp.max(top, axis=-1, keepdims=True)


def _mla_kernel(q_ref, k_ref, v_ref, o_ref):
    i = pl.program_id(2)
    heads = [slice(hh * LANES, (hh + 1) * LANES) for hh in range(q_ref.shape[1] // LANES)]

    def tile(j, carry, diag):
        ms, accs = carry
        kb = _key_rows(k_ref, j, Q_TILE)
        vb = _key_rows(v_ref, j, Q_TILE)
        scores = [_dot_t(q_ref[:, hs], kb[:, hs]) for hs in heads]
        if diag:
            r = lax.broadcasted_iota(jnp.int32, (Q_TILE, Q_TILE), 0)
            c = lax.broadcasted_iota(jnp.int32, (Q_TILE, Q_TILE), 1)
            scores = [jnp.where(r >= c, s, NEG_BIG) for s in scores]
        new_m, new_acc = [], []
        for hh, hs in enumerate(heads):
            m = jnp.maximum(ms[hh], _lane_fold_max(scores[hh]))
            p = jnp.exp2(scores[hh] - m).astype(BF16)
            new_m.append(m)
            new_acc.append(jnp.exp2(ms[hh] - m) * accs[hh] + _dot(p, vb[:, hs]))
        return tuple(new_m), tuple(new_acc)

    col = jnp.full((Q_TILE, 1), NEG_BIG, F32)
    zero = jnp.zeros((Q_TILE, LANES), F32)
    init = ((col,) * len(heads), (zero,) * len(heads))
    carry = lax.fori_loop(0, i, lambda j, cr: tile(j, cr, False), init)
    _, accs = tile(i, carry, True)
    first = lax.broadcasted_iota(jnp.int32, (Q_TILE, LANES), 1) < MLA_V
    half_turn = lambda x: pltpu.roll(x, MLA_V, 1)
    for pr in range(len(heads) // 2):
        even, odd = accs[2 * pr], accs[2 * pr + 1]
        o_ref[:, heads[pr]] = jnp.where(first, even / half_turn(even), half_turn(odd) / odd).astype(BF16)


def _sb_kernel(q_ref, k_ref, v_ref, o_ref):
    i = pl.program_id(2)
    slots = [slice(sl * LANES, (sl + 1) * LANES) for sl in range(q_ref.shape[1] // LANES)]
    first = lax.broadcasted_iota(jnp.int32, (SB_TILE, LANES), 1) < SB_DIM
    r = lax.broadcasted_iota(jnp.int32, (2 * SB_TILE, SB_TILE), 0) & (SB_TILE - 1)
    c = lax.broadcasted_iota(jnp.int32, (2 * SB_TILE, SB_TILE), 1)
    strict = r > c
    later = jnp.where(strict[:SB_TILE], 1.0, 0.0).astype(BF16)
    q2 = []
    for sl in slots:
        qp = q_ref[:, sl]
        zq = jnp.zeros_like(qp)
        q2.append(_cat_rows([jnp.where(first, qp, zq), jnp.where(first, zq, qp)]))

    def tile(j, tots, accs, diag):
        kb = _key_rows(k_ref, j, SB_TILE)
        vb = _key_rows(v_ref, j, SB_TILE)
        logits = [_dot_t(q2[n], kb[:, sl]) for n, sl in enumerate(slots)]
        drops, keeps = [], []
        for z in logits:
            lg = jnp.log2(1.0 + jnp.exp2(-jnp.abs(z)))
            drop = lg - jnp.minimum(z, 0.0)
            keep = z + drop
            drops.append(drop)
            keeps.append(jnp.where(strict, keep, 0.0) if diag else keep)
        rights = [_dot(keep.astype(BF16), later) for keep in keeps]
        new_tots, new_accs = [], []
        for n, sl in enumerate(slots):
            w = jnp.exp2(-tots[n] - (drops[n] + rights[n]))
            if diag:
                w = jnp.where(strict, w, 0.0)
            new_tots.append(tots[n] + rights[n][:, 0:1] + keeps[n][:, 0:1])
            new_accs.append(accs[n] + _dot(w.astype(BF16), vb[:, sl]))
        return tuple(new_tots), tuple(new_accs)

    def any_live(tots):
        low = tots[0]
        for tot in tots[1:]:
            low = jnp.minimum(low, tot)
        return (jnp.min(low) < SB_DEAD_LOG2).astype(jnp.int32)

    zero_tot = jnp.zeros((2 * SB_TILE, 1), F32)
    zero_acc = jnp.zeros((2 * SB_TILE, LANES), F32)
    tots, accs = tile(i, (zero_tot,) * len(slots), (zero_acc,) * len(slots), True)

    def live(state):
        t, go, _, _ = state
        return jnp.logical_and(t < i, go > 0)

    def step(state):
        t, _, tots, accs = state
        tots, accs = tile(i - 1 - t, tots, accs, False)
        return t + 1, any_live(tots), tots, accs

    _, _, _, accs = lax.while_loop(live, step, (jnp.int32(0), any_live(tots), tots, accs))
    for n, sl in enumerate(slots):
        o_ref[:, sl] = jnp.where(first, accs[n][:SB_TILE], accs[n][SB_TILE:]).astype(BF16)


def _causal_mixer(body, name, tile, pairs, q, k, v, qw, kw, vw, ow):
    b, s, _ = q.shape
    groups = v.shape[2] // (vw * pairs)
    qw, kw, vw, ow = qw * pairs, kw * pairs, vw * pairs, ow * pairs
    return pl.pallas_call(
        body,
        grid=(b, groups, s // tile),
        in_specs=[pl.BlockSpec((None, tile, qw), lambda bi, g, i: (bi, i, g)),
                  pl.BlockSpec((None, s, kw), lambda bi, g, i: (bi, 0, g)),
                  pl.BlockSpec((None, s, vw), lambda bi, g, i: (bi, 0, g))],
        out_specs=pl.BlockSpec((None, tile, ow), lambda bi, g, i: (bi, i, g)),
        out_shape=jax.ShapeDtypeStruct((b, s, groups * ow), BF16),
        compiler_params=pltpu.CompilerParams(dimension_semantics=("parallel", "parallel", "arbitrary"),
                                             vmem_limit_bytes=VMEM_LIMIT),
        name=name,
    )(q, k, v)


def _post_kernel(x_ref, oa_ref, ob_ref, om_ref, g_pre_ref, wg_ref, bg_ref, wbo_ref, wout_ref, g_post_ref,
                 g_mlp_ref, wup_ref, wdn_ref, g_mlp_post_ref, o_ref):
    x = x_ref[...]
    d = x.shape[1]
    h = _rms(x, g_pre_ref[...]).astype(BF16)
    merged = None
    for br, o_br in enumerate((oa_ref, ob_ref, om_ref)):
        sl = slice(br * d, (br + 1) * d)
        gate = jax.nn.sigmoid(_dot(h, wg_ref[:, sl]) + bg_ref[:, sl])
        yb = gate * _dot(o_br[...], wbo_ref[br])
        merged = yb if merged is None else merged + yb
    y = _dot(merged.astype(BF16), wout_ref[...])
    x1 = x + _rms(y, g_post_ref[...])

    h2 = _rms(x1, g_mlp_ref[...]).astype(BF16)
    down = None
    for c0 in range(0, wup_ref.shape[1], FF_CHUNK):
        u = jnp.maximum(_dot(h2, wup_ref[:, c0:c0 + FF_CHUNK]), 0.0)
        part = _dot((u * u).astype(BF16), wdn_ref[c0:c0 + FF_CHUNK, :])
        down = part if down is None else down + part
    o_ref[...] = x1 + _rms(down, g_mlp_post_ref[...])


def _post(x2d, oa, ob, om, g_pre, wg, bg, wbo, wout, g_post, g_mlp, wup, wdn, g_mlp_post):
    rows, d = x2d.shape
    tile = POST_TILE
    row = lambda c: pl.BlockSpec((tile, c), lambda i: (i, 0))
    consts = (g_pre, wg, bg, wbo, wout, g_post, g_mlp, wup, wdn, g_mlp_post)
    return pl.pallas_call(
        _post_kernel,
        grid=(rows // tile,),
        in_specs=[row(d), row(oa.shape[1]), row(ob.shape[1]), row(om.shape[1])] + [_const_spec(a.shape) for a in consts],
        out_specs=row(d),
        out_shape=jax.ShapeDtypeStruct((rows, d), F32),
        compiler_params=pltpu.CompilerParams(dimension_semantics=("parallel",), vmem_limit_bytes=VMEM_LIMIT),
        name="post",
    )(x2d, oa, ob, om, *consts)


def _pad_heads(w, heads, width):
    r = w.shape[0]
    w = w.reshape(r, heads, width)
    return jnp.pad(w, ((0, 0), (0, 0), (0, LANES - width))).reshape(r, heads * LANES)


def _rotate_half_cols(w_rope):
    half = MLA_ROPE // 2
    return jnp.concatenate([-w_rope[..., half:], w_rope[..., :half]], axis=-1)


def _rope_slot(w_rope):
    r, heads, _ = w_rope.shape
    return jnp.pad(w_rope, ((0, 0), (0, 0), (MLA_NOPE, LANES - MLA_NOPE - MLA_ROPE))).reshape(r, heads * LANES)


def kernel(x, mem, positions, ln_mix_pre, w_in, b_gate, q_norm, w_uq, kv_norm, w_uk, w_uv, mem_norm, w_mem_kv,
           w_branch_out, w_out, ln_mix_post, ln_mlp_pre, w_mlp_up, w_mlp_down, ln_mlp_post):
    b, s, d = x.shape
    m = mem.shape[1]
    depth = w_in.shape[0]
    half = MLA_ROPE // 2
    inv_freq = 1.0 / (ROPE_THETA ** (jnp.arange(half, dtype=F32) * (2.0 / MLA_ROPE)))
    freq = jnp.kron(jnp.eye(LANES // MLA_ROPE, dtype=F32), jnp.concatenate([inv_freq, inv_freq])[None, :])
    ones = jnp.tile((jnp.arange(LANES) >= MLA_V).astype(F32), MLA_HEADS).reshape(1, -1)
    pos2d = positions.reshape(b * s, 1)
    mem2d = mem.reshape(b * m, d)
    x2d = x.reshape(b * s, d)
    row = lambda a: a.reshape(1, -1)

    for l in range(depth):
        cq_w, ckv_w, kr_w, sb_w, qm_w, gate_w = jnp.split(w_in[l], np.cumsum([_CQ, _CKV, MLA_ROPE, _SB, _QM]).tolist(), axis=1)
        kr3 = kr_w.reshape(d, 1, MLA_ROPE)
        w1 = jnp.concatenate([cq_w, ckv_w, sb_w, qm_w, _rope_slot(kr3), _rope_slot(_rotate_half_cols(kr3)),
                              jnp.zeros((d, _W1_PAD), F32)], axis=1).astype(BF16)
        uq = w_uq[l].reshape(-1, MLA_HEADS, MLA_NOPE + MLA_ROPE)
        wuq = _pad_heads(w_uq[l], MLA_HEADS, MLA_NOPE + MLA_ROPE).astype(BF16)
        wuqr = _rope_slot(_rotate_half_cols(uq[..., MLA_NOPE:])).astype(BF16)
        wuk = _pad_heads(w_uk[l], MLA_HEADS, MLA_NOPE).astype(BF16)

        mkv = _mem_kv(mem2d, row(mem_norm[l]), w_mem_kv[l].astype(BF16))
        mkv = mkv.reshape(b, m, 2 * MEM_HEADS * MEM_DIM)

        q, k, v, sq, sk, sv, o_mem = _in_proj(
            x2d, pos2d, freq, row(ln_mix_pre[l]), w1, row(q_norm[l]), wuq, wuqr, row(kv_norm[l]), wuk,
            _pad_heads(w_uv[l], MLA_HEADS, MLA_V).astype(BF16), ones, mkv, s)

        seq = lambda a: a.reshape(b, s, -1)
        o_mla = _causal_mixer(_mla_kernel, "mla_attn", Q_TILE, MLA_PAIRS, seq(q), seq(k), seq(v),
                              2 * LANES, 2 * LANES, 2 * LANES, LANES)
        o_sb = _causal_mixer(_sb_kernel, "sb_attn", SB_TILE, SB_PAIRS, seq(sq), seq(sk), seq(sv),
                             LANES, LANES, LANES, LANES)

        x2d = _post(x2d, o_mla.reshape(b * s, -1), o_sb.reshape(b * s, -1), o_mem, row(ln_mix_pre[l]),
                    gate_w.astype(BF16), row(b_gate[l]), w_branch_out[l].astype(BF16), w_out[l].astype(BF16),
                    row(ln_mix_post[l]), row(ln_mlp_pre[l]), w_mlp_up[l].astype(BF16), w_mlp_down[l].astype(BF16),
                    row(ln_mlp_post[l]))
    return x2d.reshape(b, s, d)
```

```python
import functools

import jax
import jax.numpy as jnp
import numpy as np
from jax import lax
from jax.experimental import pallas as pl
from jax.experimental.pallas import tpu as pltpu

F32 = jnp.float32
BF16 = jnp.bfloat16

LANES = 128
EPS = 1e-6
ROPE_THETA = 10000.0

MLA_HEADS, MLA_NOPE, MLA_ROPE, MLA_V = 8, 64, 32, 64
SB_HEADS, SB_DIM = 8, 64
MEM_HEADS, MEM_DIM = 4, 128
N_BRANCH = 3
NEG_BIG = -1e30

IN_TILE, POST_TILE = 1024, 512
Q_TILE = 512
SB_TILE = 256
MLA_PAIRS, SB_PAIRS = 2, 4
SB_DEAD_LOG2 = 160.0
LOG2E = 1.4426950408889634
FF_CHUNK = 1024
VMEM_LIMIT = 60 * 1024 * 1024


def _rms(x, g):
    return x * lax.rsqrt(jnp.mean(x * x, axis=-1, keepdims=True) + EPS) * g


def _dot(a, b):
    return jnp.dot(a, b, preferred_element_type=F32)


def _dot_t(a, b):
    return lax.dot_general(a, b, (((1,), (1,)), ((), ())), preferred_element_type=F32)


def _const_spec(shape):
    nd = len(shape)
    return pl.BlockSpec(shape, lambda *_: (0,) * nd, pipeline_mode=pl.Buffered(1))


def _mem_kv_kernel(mem_ref, g_ref, w_ref, o_ref):
    h = _rms(mem_ref[...], g_ref[...]).astype(BF16)
    o_ref[...] = _dot(h, w_ref[...]).astype(BF16)


def _mem_kv(mem2d, g, w):
    rows, d = mem2d.shape
    n = w.shape[1]
    tile = 512
    return pl.pallas_call(
        _mem_kv_kernel,
        grid=(rows // tile,),
        in_specs=[pl.BlockSpec((tile, d), lambda i: (i, 0)), _const_spec((1, d)), _const_spec((d, n))],
        out_specs=pl.BlockSpec((tile, n), lambda i: (i, 0)),
        out_shape=jax.ShapeDtypeStruct((rows, n), BF16),
        compiler_params=pltpu.CompilerParams(dimension_semantics=("parallel",), vmem_limit_bytes=VMEM_LIMIT),
        name="mem_kv",
    )(mem2d, g, w)


_CQ, _CKV, _SB, _QM, _KR = 384, 256, 3 * SB_HEADS * SB_DIM, MEM_HEADS * MEM_DIM, LANES
_O_CKV = _CQ
_O_SB = _O_CKV + _CKV
_O_QM = _O_SB + _SB
_O_KR = _O_QM + _QM
_O_KRR = _O_KR + _KR
_W1 = _O_KRR + _KR
MXU_WIDTH = 256
_W1_PAD = -_W1 % MXU_WIDTH


def _in_proj_kernel(x_ref, pos_ref, freq_ref, g_ref, w1_ref, qn_ref, wuq_ref, wuqr_ref, kvn_ref, wuk_ref,
                    wuv_ref, ones_ref, mkv_ref, q_out, k_out, v_out, sq_out, sk_out, sv_out, om_out):
    groups = LANES // MLA_ROPE
    blk = pos_ref.shape[0] // groups
    pos = pos_ref[...].astype(F32)
    ang = pos[:blk] * freq_ref[0:1, :]
    for gi in range(1, groups):
        ang = ang + pos[gi * blk:(gi + 1) * blk] * freq_ref[gi:gi + 1, :]
    cos_packed, sin_packed = jnp.cos(ang), jnp.sin(ang)
    lane = lax.broadcasted_iota(jnp.int32, (blk, LANES), 1)
    rotary = jnp.logical_and(lane >= MLA_NOPE, lane < MLA_NOPE + MLA_ROPE)
    cos_blocks, sin_blocks = [], []
    for gi in range(groups):
        shift = (MLA_NOPE - gi * MLA_ROPE) % LANES
        place = (lambda t: t) if shift == 0 else (lambda t: pltpu.roll(t, shift, 1))
        cos_blocks.append(jnp.where(rotary, place(cos_packed), 1.0))
        sin_blocks.append(jnp.where(rotary, place(sin_packed), 0.0))
    cos, sin = jnp.concatenate(cos_blocks, axis=0), jnp.concatenate(sin_blocks, axis=0)

    h = _rms(x_ref[...], g_ref[...]).astype(BF16)
    proj = _dot(h, w1_ref[...])

    cq = _rms(proj[:, :_CQ], qn_ref[...]).astype(BF16)
    q = _dot(cq, wuq_ref[...])
    qr = _dot(cq, wuqr_ref[...])
    ckv = _rms(proj[:, _O_CKV:_O_SB], kvn_ref[...]).astype(BF16)
    kn = _dot(ckv, wuk_ref[...])
    v_out[...] = (_dot(ckv, wuv_ref[...]) + ones_ref[...]).astype(BF16)
    kr = proj[:, _O_KR:_O_KRR] * cos + proj[:, _O_KRR:_W1] * sin
    q_scale = (MLA_NOPE + MLA_ROPE) ** -0.5 * LOG2E
    for hd in range(MLA_HEADS):
        sl = slice(hd * LANES, (hd + 1) * LANES)
        q_out[:, sl] = ((q[:, sl] * cos + qr[:, sl] * sin) * q_scale).astype(BF16)
        k_out[:, sl] = (kn[:, sl] + kr).astype(BF16)

    w = SB_HEADS * SB_DIM
    sq_out[...] = (proj[:, _O_SB:_O_SB + w] * (SB_DIM ** -0.5 * LOG2E)).astype(BF16)
    sk_out[...] = proj[:, _O_SB + w:_O_SB + 2 * w].astype(BF16)
    sv_out[...] = proj[:, _O_SB + 2 * w:_O_QM].astype(BF16)

    for hd in range(MEM_HEADS):
        sl = slice(hd * MEM_DIM, (hd + 1) * MEM_DIM)
        vsl = slice(_QM + hd * MEM_DIM, _QM + (hd + 1) * MEM_DIM)
        qm = (proj[:, _O_QM + hd * MEM_DIM:_O_QM + (hd + 1) * MEM_DIM] * (MEM_DIM ** -0.5)).astype(BF16)
        s = _dot_t(qm, mkv_ref[:, sl])
        p = jnp.exp(s - jnp.max(s, axis=-1, keepdims=True))
        o = _dot(p.astype(BF16), mkv_ref[:, vsl])
        om_out[:, sl] = (o / jnp.sum(p, axis=-1, keepdims=True)).astype(BF16)


def _in_proj(x2d, pos2d, freq, g, w1, qn, wuq, wuqr, kvn, wuk, wuv, ones, mkv, seq):
    rows, d = x2d.shape
    tile = IN_TILE
    per_seq = seq // tile
    mem_len = mkv.shape[1]
    row = lambda c: pl.BlockSpec((tile, c), lambda i: (i, 0))
    widths = (MLA_HEADS * LANES, MLA_HEADS * LANES, MLA_HEADS * LANES, SB_HEADS * SB_DIM, SB_HEADS * SB_DIM,
              SB_HEADS * SB_DIM, MEM_HEADS * MEM_DIM)
    mem_spec = pl.BlockSpec((None, mem_len, 2 * _QM), lambda i: (i // per_seq, 0, 0))
    return pl.pallas_call(
        _in_proj_kernel,
        grid=(rows // tile,),
        in_specs=[row(d), row(1), _const_spec(freq.shape), _const_spec(g.shape), _const_spec(w1.shape),
                  _const_spec(qn.shape), _const_spec(wuq.shape), _const_spec(wuqr.shape), _const_spec(kvn.shape),
                  _const_spec(wuk.shape), _const_spec(wuv.shape), _const_spec(ones.shape), mem_spec],
        out_specs=[row(c) for c in widths],
        out_shape=[jax.ShapeDtypeStruct((rows, c), BF16) for c in widths],
        compiler_params=pltpu.CompilerParams(dimension_semantics=("parallel",), vmem_limit_bytes=VMEM_LIMIT),
        name="in_proj",
    )(x2d, pos2d, freq, g, w1, qn, wuq, wuqr, kvn, wuk, wuv, ones, mkv)


def _key_rows(ref, j, tile):
    if isinstance(j, int):
        return ref[j * tile:(j + 1) * tile, :]
    return ref[pl.ds(pl.multiple_of(j * tile, tile), tile), :]


def _cat_rows(parts):
    return jnp.concatenate(parts, axis=0)


def _lane_fold_max(x):
    top = x[:, :LANES]
    for c0 in range(LANES, x.shape[1], LANES):
        top = jnp.maximum(top, x[:, c0:c0 + LANES])
    return jnp.max(top, axis=-1, keepdims=True)


def _mla_kernel(q_ref, k_ref, v_ref, o_ref):
    i = pl.program_id(2)
    heads = [slice(hh * LANES, (hh + 1) * LANES) for hh in range(q_ref.shape[1] // LANES)]

    def tile(j, carry, diag):
        ms, accs = carry
        kb = _key_rows(k_ref, j, Q_TILE)
        vb = _key_rows(v_ref, j, Q_TILE)
        scores = [_dot_t(q_ref[:, hs], kb[:, hs]) for hs in heads]
        if diag:
            r = lax.broadcasted_iota(jnp.int32, (Q_TILE, Q_TILE), 0)
            c = lax.broadcasted_iota(jnp.int32, (Q_TILE, Q_TILE), 1)
            scores = [jnp.where(r >= c, s, NEG_BIG) for s in scores]
        new_m, new_acc = [], []
        for hh, hs in enumerate(heads):
            m = jnp.maximum(ms[hh], _lane_fold_max(scores[hh]))
            p = jnp.exp2(scores[hh] - m).astype(BF16)
            new_m.append(m)
            new_acc.append(jnp.exp2(ms[hh] - m) * accs[hh] + _dot(p, vb[:, hs]))
        return tuple(new_m), tuple(new_acc)

    def run(n_full):
        col = jnp.full((Q_TILE, 1), NEG_BIG, F32)
        zero = jnp.zeros((Q_TILE, LANES), F32)
        carry = ((col,) * len(heads), (zero,) * len(heads))
        for j in range(n_full):
            carry = tile(j, carry, False)
        _, accs = tile(n_full, carry, True)
        first = lax.broadcasted_iota(jnp.int32, (Q_TILE, LANES), 1) < MLA_V
        half_turn = lambda x: pltpu.roll(x, MLA_V, 1)
        for pr in range(len(heads) // 2):
            even, odd = accs[2 * pr], accs[2 * pr + 1]
            o_ref[:, heads[pr]] = jnp.where(first, even / half_turn(even), half_turn(odd) / odd).astype(BF16)

    for n_full in range(k_ref.shape[0] // Q_TILE):
        pl.when(i == n_full)(functools.partial(run, n_full))


def _sb_kernel(q_ref, k_ref, v_ref, o_ref):
    i = pl.program_id(2)
    slots = [slice(sl * LANES, (sl + 1) * LANES) for sl in range(q_ref.shape[1] // LANES)]
    first = lax.broadcasted_iota(jnp.int32, (SB_TILE, LANES), 1) < SB_DIM
    r = lax.broadcasted_iota(jnp.int32, (2 * SB_TILE, SB_TILE), 0) & (SB_TILE - 1)
    c = lax.broadcasted_iota(jnp.int32, (2 * SB_TILE, SB_TILE), 1)
    strict = r > c
    later = jnp.where(strict[:SB_TILE], 1.0, 0.0).astype(BF16)
    q2 = []
    for sl in slots:
        qp = q_ref[:, sl]
        zq = jnp.zeros_like(qp)
        q2.append(_cat_rows([jnp.where(first, qp, zq), jnp.where(first, zq, qp)]))

    def tile(j, tots, accs, diag):
        kb = _key_rows(k_ref, j, SB_TILE)
        vb = _key_rows(v_ref, j, SB_TILE)
        logits = [_dot_t(q2[n], kb[:, sl]) for n, sl in enumerate(slots)]
        drops, keeps = [], []
        for z in logits:
            lg = jnp.log2(1.0 + jnp.exp2(-jnp.abs(z)))
            drop = lg - jnp.minimum(z, 0.0)
            keep = z + drop
            drops.append(drop)
            keeps.append(jnp.where(strict, keep, 0.0) if diag else keep)
        rights = [_dot(keep.astype(BF16), later) for keep in keeps]
        new_tots, new_accs = [], []
        for n, sl in enumerate(slots):
            w = jnp.exp2(-tots[n] - (drops[n] + rights[n]))
            if diag:
                w = jnp.where(strict, w, 0.0)
            new_tots.append(tots[n] + rights[n][:, 0:1] + keeps[n][:, 0:1])
            new_accs.append(accs[n] + _dot(w.astype(BF16), vb[:, sl]))
        return tuple(new_tots), tuple(new_accs)

    def any_live(tots):
        low = tots[0]
        for tot in tots[1:]:
            low = jnp.minimum(low, tot)
        return (jnp.min(low) < SB_DEAD_LOG2).astype(jnp.int32)

    def live(state):
        t, go, _, _ = state
        return jnp.logical_and(t < i, go > 0)

    def step(state):
        t, _, tots, accs = state
        tots, accs = tile(i - 1 - t, tots, accs, False)
        return t + 1, any_live(tots), tots, accs

    def run(n_older):
        zero_tot = jnp.zeros((2 * SB_TILE, 1), F32)
        zero_acc = jnp.zeros((2 * SB_TILE, LANES), F32)
        tots, accs = tile(i, (zero_tot,) * len(slots), (zero_acc,) * len(slots), True)
        for t in range(n_older):
            tots, accs = tile(i - 1 - t, tots, accs, False)
        if n_older:
            _, _, _, accs = lax.while_loop(live, step, (jnp.int32(n_older), any_live(tots), tots, accs))
        for n, sl in enumerate(slots):
            o_ref[:, sl] = jnp.where(first, accs[n][:SB_TILE], accs[n][SB_TILE:]).astype(BF16)

    pl.when(i == 0)(functools.partial(run, 0))
    pl.when(i > 0)(functools.partial(run, 1))


def _causal_mixer(body, name, tile, pairs, q, k, v, qw, kw, vw, ow):
    b, s, _ = q.shape
    groups = v.shape[2] // (vw * pairs)
    qw, kw, vw, ow = qw * pairs, kw * pairs, vw * pairs, ow * pairs
    return pl.pallas_call(
        body,
        grid=(b, groups, s // tile),
        in_specs=[pl.BlockSpec((None, tile, qw), lambda bi, g, i: (bi, i, g)),
                  pl.BlockSpec((None, s, kw), lambda bi, g, i: (bi, 0, g)),
                  pl.BlockSpec((None, s, vw), lambda bi, g, i: (bi, 0, g))],
        out_specs=pl.BlockSpec((None, tile, ow), lambda bi, g, i: (bi, i, g)),
        out_shape=jax.ShapeDtypeStruct((b, s, groups * ow), BF16),
        compiler_params=pltpu.CompilerParams(dimension_semantics=("parallel", "parallel", "arbitrary"),
                                             vmem_limit_bytes=VMEM_LIMIT),
        name=name,
    )(q, k, v)


def _post_kernel(x_ref, oa_ref, ob_ref, om_ref, g_pre_ref, wg_ref, bg_ref, wbo_ref, wout_ref, g_post_ref,
                 g_mlp_ref, wup_ref, wdn_ref, g_mlp_post_ref, o_ref):
    x = x_ref[...]
    d = x.shape[1]
    h = _rms(x, g_pre_ref[...]).astype(BF16)
    merged = None
    for br, o_br in enumerate((oa_ref, ob_ref, om_ref)):
        sl = slice(br * d, (br + 1) * d)
        gate = jax.nn.sigmoid(_dot(h, wg_ref[:, sl]) + bg_ref[:, sl])
        yb = gate * _dot(o_br[...], wbo_ref[br])
        merged = yb if merged is None else merged + yb
    y = _dot(merged.astype(BF16), wout_ref[...])
    x1 = x + _rms(y, g_post_ref[...])

    h2 = _rms(x1, g_mlp_ref[...]).astype(BF16)
    down = None
    for c0 in range(0, wup_ref.shape[1], FF_CHUNK):
        u = jnp.maximum(_dot(h2, wup_ref[:, c0:c0 + FF_CHUNK]), 0.0)
        part = _dot((u * u).astype(BF16), wdn_ref[c0:c0 + FF_CHUNK, :])
        down = part if down is None else down + part
    o_ref[...] = x1 + _rms(down, g_mlp_post_ref[...])


def _post(x2d, oa, ob, om, g_pre, wg, bg, wbo, wout, g_post, g_mlp, wup, wdn, g_mlp_post):
    rows, d = x2d.shape
    tile = POST_TILE
    row = lambda c: pl.BlockSpec((tile, c), lambda i: (i, 0))
    consts = (g_pre, wg, bg, wbo, wout, g_post, g_mlp, wup, wdn, g_mlp_post)
    return pl.pallas_call(
        _post_kernel,
        grid=(rows // tile,),
        in_specs=[row(d), row(oa.shape[1]), row(ob.shape[1]), row(om.shape[1])] + [_const_spec(a.shape) for a in consts],
        out_specs=row(d),
        out_shape=jax.ShapeDtypeStruct((rows, d), F32),
        compiler_params=pltpu.CompilerParams(dimension_semantics=("parallel",), vmem_limit_bytes=VMEM_LIMIT),
        name="post",
    )(x2d, oa, ob, om, *consts)


def _pad_heads(w, heads, width):
    r = w.shape[0]
    w = w.reshape(r, heads, width)
    return jnp.pad(w, ((0, 0), (0, 0), (0, LANES - width))).reshape(r, heads * LANES)


def _rotate_half_cols(w_rope):
    half = MLA_ROPE // 2
    return jnp.concatenate([-w_rope[..., half:], w_rope[..., :half]], axis=-1)


def _rope_slot(w_rope):
    r, heads, _ = w_rope.shape
    return jnp.pad(w_rope, ((0, 0), (0, 0), (MLA_NOPE, LANES - MLA_NOPE - MLA_ROPE))).reshape(r, heads * LANES)


def kernel(x, mem, positions, ln_mix_pre, w_in, b_gate, q_norm, w_uq, kv_norm, w_uk, w_uv, mem_norm, w_mem_kv,
           w_branch_out, w_out, ln_mix_post, ln_mlp_pre, w_mlp_up, w_mlp_down, ln_mlp_post):
    b, s, d = x.shape
    m = mem.shape[1]
    depth = w_in.shape[0]
    half = MLA_ROPE // 2
    inv_freq = 1.0 / (ROPE_THETA ** (jnp.arange(half, dtype=F32) * (2.0 / MLA_ROPE)))
    freq = jnp.kron(jnp.eye(LANES // MLA_ROPE, dtype=F32), jnp.concatenate([inv_freq, inv_freq])[None, :])
    ones = jnp.tile((jnp.arange(LANES) >= MLA_V).astype(F32), MLA_HEADS).reshape(1, -1)
    pos2d = positions.reshape(b * s, 1)
    mem2d = mem.reshape(b * m, d)
    x2d = x.reshape(b * s, d)
    row = lambda a: a.reshape(1, -1)

    for l in range(depth):
        cq_w, ckv_w, kr_w, sb_w, qm_w, gate_w = jnp.split(w_in[l], np.cumsum([_CQ, _CKV, MLA_ROPE, _SB, _QM]).tolist(), axis=1)
        kr3 = kr_w.reshape(d, 1, MLA_ROPE)
        w1 = jnp.concatenate([cq_w, ckv_w, sb_w, qm_w, _rope_slot(kr3), _rope_slot(_rotate_half_cols(kr3)),
                              jnp.zeros((d, _W1_PAD), F32)], axis=1).astype(BF16)
        uq = w_uq[l].reshape(-1, MLA_HEADS, MLA_NOPE + MLA_ROPE)
        wuq = _pad_heads(w_uq[l], MLA_HEADS, MLA_NOPE + MLA_ROPE).astype(BF16)
        wuqr = _rope_slot(_rotate_half_cols(uq[..., MLA_NOPE:])).astype(BF16)
        wuk = _pad_heads(w_uk[l], MLA_HEADS, MLA_NOPE).astype(BF16)

        mkv = _mem_kv(mem2d, row(mem_norm[l]), w_mem_kv[l].astype(BF16))
        mkv = mkv.reshape(b, m, 2 * MEM_HEADS * MEM_DIM)

        q, k, v, sq, sk, sv, o_mem = _in_proj(
            x2d, pos2d, freq, row(ln_mix_pre[l]), w1, row(q_norm[l]), wuq, wuqr, row(kv_norm[l]), wuk,
            _pad_heads(w_uv[l], MLA_HEADS, MLA_V).astype(BF16), ones, mkv, s)

        seq = lambda a: a.reshape(b, s, -1)
        o_mla = _causal_mixer(_mla_kernel, "mla_attn", Q_TILE, MLA_PAIRS, seq(q), seq(k), seq(v),
                              2 * LANES, 2 * LANES, 2 * LANES, LANES)
        o_sb = _causal_mixer(_sb_kernel, "sb_attn", SB_TILE, SB_PAIRS, seq(sq), seq(sk), seq(sv),
                             LANES, LANES, LANES, LANES)

        x2d = _post(x2d, o_mla.reshape(b * s, -1), o_sb.reshape(b * s, -1), o_mem, row(ln_mix_pre[l]),
                    gate_w.astype(BF16), row(b_gate[l]), w_branch_out[l].astype(BF16), w_out[l].astype(BF16),
                    row(ln_mix_post[l]), row(ln_mlp_pre[l]), w_mlp_up[l].astype(BF16), w_mlp_down[l].astype(BF16),
                    row(ln_mlp_post[l]))
    return x2d.reshape(b, s, d)
```

```python
import functools

import jax
import jax.numpy as jnp
import numpy as np
from jax import lax
from jax.experimental import pallas as pl
from jax.experimental.pallas import tpu as pltpu

F32 = jnp.float32
BF16 = jnp.bfloat16

LANES = 128
EPS = 1e-6
ROPE_THETA = 10000.0

MLA_HEADS, MLA_NOPE, MLA_ROPE, MLA_V = 8, 64, 32, 64
SB_HEADS, SB_DIM = 8, 64
MEM_HEADS, MEM_DIM = 4, 128
N_BRANCH = 3
NEG_BIG = -1e30

IN_TILE, POST_TILE = 1024, 512
Q_TILE = 512
SB_TILE = 256
MLA_PAIRS, SB_PAIRS = 2, 4
SB_DEAD_LOG2 = 160.0
LOG2E = 1.4426950408889634
FF_CHUNK = 1024
VMEM_LIMIT = 60 * 1024 * 1024


def _rms(x, g):
    return x * lax.rsqrt(jnp.mean(x * x, axis=-1, keepdims=True) + EPS) * g


def _dot(a, b):
    return jnp.dot(a, b, preferred_element_type=F32)


def _dot_t(a, b):
    return lax.dot_general(a, b, (((1,), (1,)), ((), ())), preferred_element_type=F32)


def _const_spec(shape):
    nd = len(shape)
    return pl.BlockSpec(shape, lambda *_: (0,) * nd, pipeline_mode=pl.Buffered(1))


def _mem_kv_kernel(mem_ref, g_ref, w_ref, o_ref):
    h = _rms(mem_ref[...], g_ref[...]).astype(BF16)
    o_ref[...] = _dot(h, w_ref[...]).astype(BF16)


def _mem_kv(mem2d, g, w):
    rows, d = mem2d.shape
    n = w.shape[1]
    tile = 512
    return pl.pallas_call(
        _mem_kv_kernel,
        grid=(rows // tile,),
        in_specs=[pl.BlockSpec((tile, d), lambda i: (i, 0)), _const_spec((1, d)), _const_spec((d, n))],
        out_specs=pl.BlockSpec((tile, n), lambda i: (i, 0)),
        out_shape=jax.ShapeDtypeStruct((rows, n), BF16),
        compiler_params=pltpu.CompilerParams(dimension_semantics=("parallel",), vmem_limit_bytes=VMEM_LIMIT),
        name="mem_kv",
    )(mem2d, g, w)


_CQ, _CKV, _SB, _QM, _KR = 384, 256, 3 * SB_HEADS * SB_DIM, MEM_HEADS * MEM_DIM, LANES
_O_CKV = _CQ
_O_SB = _O_CKV + _CKV
_O_QM = _O_SB + _SB
_O_KR = _O_QM + _QM
_O_KRR = _O_KR + _KR
_W1 = _O_KRR + _KR
MXU_WIDTH = 256
_W1_PAD = -_W1 % MXU_WIDTH


def _in_proj_kernel(x_ref, pos_ref, freq_ref, g_ref, w1_ref, qn_ref, wuq_ref, wuqr_ref, kvn_ref, wuk_ref,
                    wuv_ref, ones_ref, mkv_ref, q_out, k_out, v_out, sq_out, sk_out, sv_out, om_out):
    groups = LANES // MLA_ROPE
    blk = pos_ref.shape[0]
    ang = pos_ref[...] * freq_ref[...]
    cos_packed, sin_packed = jnp.cos(ang), jnp.sin(ang)
    lane = lax.broadcasted_iota(jnp.int32, (blk, LANES), 1)
    rotary = jnp.logical_and(lane >= MLA_NOPE, lane < MLA_NOPE + MLA_ROPE)
    cos_blocks, sin_blocks = [], []
    for gi in range(groups):
        shift = (MLA_NOPE - gi * MLA_ROPE) % LANES
        place = (lambda t: t) if shift == 0 else (lambda t: pltpu.roll(t, shift, 1))
        cos_blocks.append(jnp.where(rotary, place(cos_packed), 1.0))
        sin_blocks.append(jnp.where(rotary, place(sin_packed), 0.0))
    cos, sin = jnp.concatenate(cos_blocks, axis=0), jnp.concatenate(sin_blocks, axis=0)

    h = _rms(x_ref[...], g_ref[...]).astype(BF16)
    proj = _dot(h, w1_ref[...])

    cq = _rms(proj[:, :_CQ], qn_ref[...]).astype(BF16)
    q = _dot(cq, wuq_ref[...])
    qr = _dot(cq, wuqr_ref[...])
    ckv = _rms(proj[:, _O_CKV:_O_SB], kvn_ref[...]).astype(BF16)
    kn = _dot(ckv, wuk_ref[...])
    v_out[...] = (_dot(ckv, wuv_ref[...]) + ones_ref[...]).astype(BF16)
    kr = proj[:, _O_KR:_O_KRR] * cos + proj[:, _O_KRR:_W1] * sin
    q_scale = (MLA_NOPE + MLA_ROPE) ** -0.5 * LOG2E
    for hd in range(MLA_HEADS):
        sl = slice(hd * LANES, (hd + 1) * LANES)
        q_out[:, sl] = ((q[:, sl] * cos + qr[:, sl] * sin) * q_scale).astype(BF16)
        k_out[:, sl] = (kn[:, sl] + kr).astype(BF16)

    w = SB_HEADS * SB_DIM
    sq_out[...] = (proj[:, _O_SB:_O_SB + w] * (SB_DIM ** -0.5 * LOG2E)).astype(BF16)
    sk_out[...] = proj[:, _O_SB + w:_O_SB + 2 * w].astype(BF16)
    sv_out[...] = proj[:, _O_SB + 2 * w:_O_QM].astype(BF16)

    for hd in range(MEM_HEADS):
        sl = slice(hd * MEM_DIM, (hd + 1) * MEM_DIM)
        vsl = slice(_QM + hd * MEM_DIM, _QM + (hd + 1) * MEM_DIM)
        qm = (proj[:, _O_QM + hd * MEM_DIM:_O_QM + (hd + 1) * MEM_DIM] * (MEM_DIM ** -0.5)).astype(BF16)
        s = _dot_t(qm, mkv_ref[:, sl])
        p = jnp.exp(s - jnp.max(s, axis=-1, keepdims=True))
        o = _dot(p.astype(BF16), mkv_ref[:, vsl])
        om_out[:, sl] = (o / jnp.sum(p, axis=-1, keepdims=True)).astype(BF16)


def _in_proj(x2d, pos2d, freq, g, w1, qn, wuq, wuqr, kvn, wuk, wuv, ones, mkv, seq):
    rows, d = x2d.shape
    tile = IN_TILE
    per_seq = seq // tile
    mem_len = mkv.shape[1]
    row = lambda c: pl.BlockSpec((tile, c), lambda i: (i, 0))
    widths = (MLA_HEADS * LANES, MLA_HEADS * LANES, MLA_HEADS * LANES, SB_HEADS * SB_DIM, SB_HEADS * SB_DIM,
              SB_HEADS * SB_DIM, MEM_HEADS * MEM_DIM)
    mem_spec = pl.BlockSpec((None, mem_len, 2 * _QM), lambda i: (i // per_seq, 0, 0))
    return pl.pallas_call(
        _in_proj_kernel,
        grid=(rows // tile,),
        in_specs=[row(d), pl.BlockSpec((tile * MLA_ROPE // LANES, LANES), lambda i: (i, 0)), _const_spec(freq.shape), _const_spec(g.shape), _const_spec(w1.shape),
                  _const_spec(qn.shape), _const_spec(wuq.shape), _const_spec(wuqr.shape), _const_spec(kvn.shape),
                  _const_spec(wuk.shape), _const_spec(wuv.shape), _const_spec(ones.shape), mem_spec],
        out_specs=[row(c) for c in widths],
        out_shape=[jax.ShapeDtypeStruct((rows, c), BF16) for c in widths],
        compiler_params=pltpu.CompilerParams(dimension_semantics=("parallel",), vmem_limit_bytes=VMEM_LIMIT),
        name="in_proj",
    )(x2d, pos2d, freq, g, w1, qn, wuq, wuqr, kvn, wuk, wuv, ones, mkv)


def _key_rows(ref, j, tile):
    if isinstance(j, int):
        return ref[j * tile:(j + 1) * tile, :]
    return ref[pl.ds(pl.multiple_of(j * tile, tile), tile), :]


def _cat_rows(parts):
    return jnp.concatenate(parts, axis=0)


def _lane_fold_max(x):
    top = x[:, :LANES]
    for c0 in range(LANES, x.shape[1], LANES):
        top = jnp.maximum(top, x[:, c0:c0 + LANES])
    return jnp.max(top, axis=-1, keepdims=True)


def _mla_kernel(q_ref, k_ref, v_ref, o_ref):
    i = pl.program_id(2)
    heads = [slice(hh * LANES, (hh + 1) * LANES) for hh in range(q_ref.shape[1] // LANES)]

    def tile(j, carry, diag):
        ms, accs = carry
        kb = _key_rows(k_ref, j, Q_TILE)
        vb = _key_rows(v_ref, j, Q_TILE)
        scores = [_dot_t(q_ref[:, hs], kb[:, hs]) for hs in heads]
        if diag:
            r = lax.broadcasted_iota(jnp.int32, (Q_TILE, Q_TILE), 0)
            c = lax.broadcasted_iota(jnp.int32, (Q_TILE, Q_TILE), 1)
            scores = [jnp.where(r >= c, s, NEG_BIG) for s in scores]
        new_m, new_acc = [], []
        for hh, hs in enumerate(heads):
            m = jnp.maximum(ms[hh], _lane_fold_max(scores[hh]))
            p = jnp.exp2(scores[hh] - m).astype(BF16)
            new_m.append(m)
            new_acc.append(jnp.exp2(ms[hh] - m) * accs[hh] + _dot(p, vb[:, hs]))
        return tuple(new_m), tuple(new_acc)

    def run(n_full):
        col = jnp.full((Q_TILE, 1), NEG_BIG, F32)
        zero = jnp.zeros((Q_TILE, LANES), F32)
        carry = ((col,) * len(heads), (zero,) * len(heads))
        for j in range(n_full):
            carry = tile(j, carry, False)
        _, accs = tile(n_full, carry, True)
        first = lax.broadcasted_iota(jnp.int32, (Q_TILE, LANES), 1) < MLA_V
        half_turn = lambda x: pltpu.roll(x, MLA_V, 1)
        for pr in range(len(heads) // 2):
            even, odd = accs[2 * pr], accs[2 * pr + 1]
            o_ref[:, heads[pr]] = jnp.where(first, even / half_turn(even), half_turn(odd) / odd).astype(BF16)

    for n_full in range(k_ref.shape[0] // Q_TILE):
        pl.when(i == n_full)(functools.partial(run, n_full))


def _sb_kernel(q_ref, k_ref, v_ref, o_ref):
    i = pl.program_id(2)
    slots = [slice(sl * LANES, (sl + 1) * LANES) for sl in range(q_ref.shape[1] // LANES)]
    first = lax.broadcasted_iota(jnp.int32, (SB_TILE, LANES), 1) < SB_DIM
    r = lax.broadcasted_iota(jnp.int32, (2 * SB_TILE, SB_TILE), 0) & (SB_TILE - 1)
    c = lax.broadcasted_iota(jnp.int32, (2 * SB_TILE, SB_TILE), 1)
    strict = r > c
    later = jnp.where(strict[:SB_TILE], 1.0, 0.0).astype(BF16)
    q2 = []
    for sl in slots:
        qp = q_ref[:, sl]
        zq = jnp.zeros_like(qp)
        q2.append(_cat_rows([jnp.where(first, qp, zq), jnp.where(first, zq, qp)]))

    def tile(j, tots, accs, diag):
        kb = _key_rows(k_ref, j, SB_TILE)
        vb = _key_rows(v_ref, j, SB_TILE)
        logits = [_dot_t(q2[n], kb[:, sl]) for n, sl in enumerate(slots)]
        drops, keeps = [], []
        for z in logits:
            lg = jnp.log2(1.0 + jnp.exp2(-jnp.abs(z)))
            drop = lg - jnp.minimum(z, 0.0)
            keep = z + drop
            drops.append(drop)
            keeps.append(jnp.where(strict, keep, 0.0) if diag else keep)
        rights = [_dot(keep.astype(BF16), later) for keep in keeps]
        new_tots, new_accs = [], []
        for n, sl in enumerate(slots):
            w = jnp.exp2(-tots[n] - (drops[n] + rights[n]))
            if diag:
                w = jnp.where(strict, w, 0.0)
            new_tots.append(tots[n] + rights[n][:, 0:1] + keeps[n][:, 0:1])
            new_accs.append(accs[n] + _dot(w.astype(BF16), vb[:, sl]))
        return tuple(new_tots), tuple(new_accs)

    def any_live(tots):
        low = tots[0]
        for tot in tots[1:]:
            low = jnp.minimum(low, tot)
        return (jnp.min(low) < SB_DEAD_LOG2).astype(jnp.int32)

    def live(state):
        t, go, _, _ = state
        return jnp.logical_and(t < i, go > 0)

    def step(state):
        t, _, tots, accs = state
        tots, accs = tile(i - 1 - t, tots, accs, False)
        return t + 1, any_live(tots), tots, accs

    def run(n_older):
        zero_tot = jnp.zeros((2 * SB_TILE, 1), F32)
        zero_acc = jnp.zeros((2 * SB_TILE, LANES), F32)
        tots, accs = tile(i, (zero_tot,) * len(slots), (zero_acc,) * len(slots), True)
        for t in range(n_older):
            tots, accs = tile(i - 1 - t, tots, accs, False)
        if n_older:
            _, _, _, accs = lax.while_loop(live, step, (jnp.int32(n_older), any_live(tots), tots, accs))
        for n, sl in enumerate(slots):
            o_ref[:, sl] = jnp.where(first, accs[n][:SB_TILE], accs[n][SB_TILE:]).astype(BF16)

    pl.when(i == 0)(functools.partial(run, 0))
    pl.when(i > 0)(functools.partial(run, 1))


def _causal_mixer(body, name, tile, pairs, q, k, v, qw, kw, vw, ow):
    b, s, _ = q.shape
    groups = v.shape[2] // (vw * pairs)
    qw, kw, vw, ow = qw * pairs, kw * pairs, vw * pairs, ow * pairs
    return pl.pallas_call(
        body,
        grid=(b, groups, s // tile),
        in_specs=[pl.BlockSpec((None, tile, qw), lambda bi, g, i: (bi, i, g)),
                  pl.BlockSpec((None, s, kw), lambda bi, g, i: (bi, 0, g)),
                  pl.BlockSpec((None, s, vw), lambda bi, g, i: (bi, 0, g))],
        out_specs=pl.BlockSpec((None, tile, ow), lambda bi, g, i: (bi, i, g)),
        out_shape=jax.ShapeDtypeStruct((b, s, groups * ow), BF16),
        compiler_params=pltpu.CompilerParams(dimension_semantics=("parallel", "parallel", "arbitrary"),
                                             vmem_limit_bytes=VMEM_LIMIT),
        name=name,
    )(q, k, v)


def _post_kernel(x_ref, oa_ref, ob_ref, om_ref, g_pre_ref, wg_ref, bg_ref, wbo_ref, wout_ref, g_post_ref,
                 g_mlp_ref, wup_ref, wdn_ref, g_mlp_post_ref, o_ref):
    x = x_ref[...]
    d = x.shape[1]
    h = _rms(x, g_pre_ref[...]).astype(BF16)
    merged = None
    for br, o_br in enumerate((oa_ref, ob_ref, om_ref)):
        sl = slice(br * d, (br + 1) * d)
        gate = jax.nn.sigmoid(_dot(h, wg_ref[:, sl]) + bg_ref[:, sl])
        yb = gate * _dot(o_br[...], wbo_ref[br])
        merged = yb if merged is None else merged + yb
    y = _dot(merged.astype(BF16), wout_ref[...])
    x1 = x + _rms(y, g_post_ref[...])

    h2 = _rms(x1, g_mlp_ref[...]).astype(BF16)
    down = None
    for c0 in range(0, wup_ref.shape[1], FF_CHUNK):
        u = jnp.maximum(_dot(h2, wup_ref[:, c0:c0 + FF_CHUNK]), 0.0)
        part = _dot((u * u).astype(BF16), wdn_ref[c0:c0 + FF_CHUNK, :])
        down = part if down is None else down + part
    o_ref[...] = x1 + _rms(down, g_mlp_post_ref[...])


def _post(x2d, oa, ob, om, g_pre, wg, bg, wbo, wout, g_post, g_mlp, wup, wdn, g_mlp_post):
    rows, d = x2d.shape
    tile = POST_TILE
    row = lambda c: pl.BlockSpec((tile, c), lambda i: (i, 0))
    consts = (g_pre, wg, bg, wbo, wout, g_post, g_mlp, wup, wdn, g_mlp_post)
    return pl.pallas_call(
        _post_kernel,
        grid=(rows // tile,),
        in_specs=[row(d), row(oa.shape[1]), row(ob.shape[1]), row(om.shape[1])] + [_const_spec(a.shape) for a in consts],
        out_specs=row(d),
        out_shape=jax.ShapeDtypeStruct((rows, d), F32),
        compiler_params=pltpu.CompilerParams(dimension_semantics=("parallel",), vmem_limit_bytes=VMEM_LIMIT),
        name="post",
    )(x2d, oa, ob, om, *consts)


def _pad_heads(w, heads, width):
    r = w.shape[0]
    w = w.reshape(r, heads, width)
    return jnp.pad(w, ((0, 0), (0, 0), (0, LANES - width))).reshape(r, heads * LANES)


def _rotate_half_cols(w_rope):
    half = MLA_ROPE // 2
    return jnp.concatenate([-w_rope[..., half:], w_rope[..., :half]], axis=-1)


def _rope_slot(w_rope):
    r, heads, _ = w_rope.shape
    return jnp.pad(w_rope, ((0, 0), (0, 0), (MLA_NOPE, LANES - MLA_NOPE - MLA_ROPE))).reshape(r, heads * LANES)


def kernel(x, mem, positions, ln_mix_pre, w_in, b_gate, q_norm, w_uq, kv_norm, w_uk, w_uv, mem_norm, w_mem_kv,
           w_branch_out, w_out, ln_mix_post, ln_mlp_pre, w_mlp_up, w_mlp_down, ln_mlp_post):
    b, s, d = x.shape
    m = mem.shape[1]
    depth = w_in.shape[0]
    half = MLA_ROPE // 2
    inv_freq = 1.0 / (ROPE_THETA ** (jnp.arange(half, dtype=F32) * (2.0 / MLA_ROPE)))
    groups = LANES // MLA_ROPE
    freq = jnp.tile(jnp.concatenate([inv_freq, inv_freq]), groups).reshape(1, LANES)
    ones = jnp.tile((jnp.arange(LANES) >= MLA_V).astype(F32), MLA_HEADS).reshape(1, -1)
    pos2d = positions.astype(F32).reshape(b * s // IN_TILE, groups, IN_TILE // groups).swapaxes(1, 2)
    pos2d = jnp.repeat(pos2d, MLA_ROPE, axis=2).reshape(b * s // groups, LANES)
    mem2d = mem.reshape(b * m, d)
    x2d = x.reshape(b * s, d)
    row = lambda a: a.reshape(1, -1)

    for l in range(depth):
        cq_w, ckv_w, kr_w, sb_w, qm_w, gate_w = jnp.split(w_in[l], np.cumsum([_CQ, _CKV, MLA_ROPE, _SB, _QM]).tolist(), axis=1)
        kr3 = kr_w.reshape(d, 1, MLA_ROPE)
        w1 = jnp.concatenate([cq_w, ckv_w, sb_w, qm_w, _rope_slot(kr3), _rope_slot(_rotate_half_cols(kr3)),
                              jnp.zeros((d, _W1_PAD), F32)], axis=1).astype(BF16)
        uq = w_uq[l].reshape(-1, MLA_HEADS, MLA_NOPE + MLA_ROPE)
        wuq = _pad_heads(w_uq[l], MLA_HEADS, MLA_NOPE + MLA_ROPE).astype(BF16)
        wuqr = _rope_slot(_rotate_half_cols(uq[..., MLA_NOPE:])).astype(BF16)
        wuk = _pad_heads(w_uk[l], MLA_HEADS, MLA_NOPE).astype(BF16)

        mkv = _mem_kv(mem2d, row(mem_norm[l]), w_mem_kv[l].astype(BF16))
        mkv = mkv.reshape(b, m, 2 * MEM_HEADS * MEM_DIM)

        q, k, v, sq, sk, sv, o_mem = _in_proj(
            x2d, pos2d, freq, row(ln_mix_pre[l]), w1, row(q_norm[l]), wuq, wuqr, row(kv_norm[l]), wuk,
            _pad_heads(w_uv[l], MLA_HEADS, MLA_V).astype(BF16), ones, mkv, s)

        seq = lambda a: a.reshape(b, s, -1)
        o_mla = _causal_mixer(_mla_kernel, "mla_attn", Q_TILE, MLA_PAIRS, seq(q), seq(k), seq(v),
                              2 * LANES, 2 * LANES, 2 * LANES, LANES)
        o_sb = _causal_mixer(_sb_kernel, "sb_attn", SB_TILE, SB_PAIRS, seq(sq), seq(sk), seq(sv),
                             LANES, LANES, LANES, LANES)

        x2d = _post(x2d, o_mla.reshape(b * s, -1), o_sb.reshape(b * s, -1), o_mem, row(ln_mix_pre[l]),
                    gate_w.astype(BF16), row(b_gate[l]), w_branch_out[l].astype(BF16), w_out[l].astype(BF16),
                    row(ln_mix_post[l]), row(ln_mlp_pre[l]), w_mlp_up[l].astype(BF16), w_mlp_down[l].astype(BF16),
                    row(ln_mlp_post[l]))
    return x2d.reshape(b, s, d)
```

```python
import functools

import jax
import jax.numpy as jnp
import numpy as np
from jax import lax
from jax.experimental import pallas as pl
from jax.experimental.pallas import tpu as pltpu

F32 = jnp.float32
BF16 = jnp.bfloat16

LANES = 128
EPS = 1e-6
ROPE_THETA = 10000.0

MLA_HEADS, MLA_NOPE, MLA_ROPE, MLA_V = 8, 64, 32, 64
SB_HEADS, SB_DIM = 8, 64
MEM_HEADS, MEM_DIM = 4, 128
N_BRANCH = 3
NEG_BIG = -1e30

IN_TILE, POST_TILE = 1024, 512
Q_TILE = 512
SB_TILE = 256
MLA_PAIRS, SB_PAIRS = 2, 4
SB_DEAD_LOG2 = 160.0
LOG2E = 1.4426950408889634
FF_CHUNK = 1024
VMEM_LIMIT = 60 * 1024 * 1024


def _rms(x, g):
    return x * lax.rsqrt(jnp.mean(x * x, axis=-1, keepdims=True) + EPS) * g


def _dot(a, b):
    return jnp.dot(a, b, preferred_element_type=F32)


def _dot_t(a, b):
    return lax.dot_general(a, b, (((1,), (1,)), ((), ())), preferred_element_type=F32)


def _const_spec(shape):
    nd = len(shape)
    return pl.BlockSpec(shape, lambda *_: (0,) * nd, pipeline_mode=pl.Buffered(1))


def _mem_kv_kernel(mem_ref, g_ref, w_ref, o_ref):
    h = _rms(mem_ref[...], g_ref[...]).astype(BF16)
    o_ref[...] = _dot(h, w_ref[...]).astype(BF16)


def _mem_kv(mem2d, g, w):
    rows, d = mem2d.shape
    n = w.shape[1]
    tile = 512
    return pl.pallas_call(
        _mem_kv_kernel,
        grid=(rows // tile,),
        in_specs=[pl.BlockSpec((tile, d), lambda i: (i, 0)), _const_spec((1, d)), _const_spec((d, n))],
        out_specs=pl.BlockSpec((tile, n), lambda i: (i, 0)),
        out_shape=jax.ShapeDtypeStruct((rows, n), BF16),
        compiler_params=pltpu.CompilerParams(dimension_semantics=("parallel",), vmem_limit_bytes=VMEM_LIMIT),
        name="mem_kv",
    )(mem2d, g, w)


_CQ, _CKV, _SB, _QM, _KR = 384, 256, 3 * SB_HEADS * SB_DIM, MEM_HEADS * MEM_DIM, LANES
_O_CKV = _CQ
_O_SB = _O_CKV + _CKV
_O_QM = _O_SB + _SB
_O_KR = _O_QM + _QM
_O_KRR = _O_KR + _KR
_W1 = _O_KRR + _KR
MXU_WIDTH = 256
_W1_PAD = -_W1 % MXU_WIDTH


def _in_proj_kernel(x_ref, pos_ref, freq_ref, g_ref, w1_ref, qn_ref, wuq_ref, wuqr_ref, kvn_ref, wuk_ref,
                    wuv_ref, ones_ref, mkv_ref, q_out, k_out, v_out, sq_out, sk_out, sv_out, om_out):
    groups = LANES // MLA_ROPE
    blk = pos_ref.shape[0]
    ang = pos_ref[...] * freq_ref[...]
    cos_packed, sin_packed = jnp.cos(ang), jnp.sin(ang)
    lane = lax.broadcasted_iota(jnp.int32, (blk, LANES), 1)
    rotary = jnp.logical_and(lane >= MLA_NOPE, lane < MLA_NOPE + MLA_ROPE)
    cos_blocks, sin_blocks = [], []
    for gi in range(groups):
        shift = (MLA_NOPE - gi * MLA_ROPE) % LANES
        place = (lambda t: t) if shift == 0 else (lambda t: pltpu.roll(t, shift, 1))
        cos_blocks.append(jnp.where(rotary, place(cos_packed), 1.0))
        sin_blocks.append(jnp.where(rotary, place(sin_packed), 0.0))
    cos, sin = jnp.concatenate(cos_blocks, axis=0), jnp.concatenate(sin_blocks, axis=0)

    h = _rms(x_ref[...], g_ref[...]).astype(BF16)
    proj = _dot(h, w1_ref[...])

    cq = _rms(proj[:, :_CQ], qn_ref[...]).astype(BF16)
    q = _dot(cq, wuq_ref[...])
    qr = _dot(cq, wuqr_ref[...])
    ckv = _rms(proj[:, _O_CKV:_O_SB], kvn_ref[...]).astype(BF16)
    kn = _dot(ckv, wuk_ref[...])
    v_out[...] = (_dot(ckv, wuv_ref[...]) + ones_ref[...]).astype(BF16)
    kr = proj[:, _O_KR:_O_KRR] * cos + proj[:, _O_KRR:_W1] * sin
    q_scale = (MLA_NOPE + MLA_ROPE) ** -0.5 * LOG2E
    for hd in range(MLA_HEADS):
        sl = slice(hd * LANES, (hd + 1) * LANES)
        q_out[:, sl] = ((q[:, sl] * cos + qr[:, sl] * sin) * q_scale).astype(BF16)
        k_out[:, sl] = (kn[:, sl] + kr).astype(BF16)

    w = SB_HEADS * SB_DIM
    sq_out[...] = (proj[:, _O_SB:_O_SB + w] * (SB_DIM ** -0.5 * LOG2E)).astype(BF16)
    sk_out[...] = proj[:, _O_SB + w:_O_SB + 2 * w].astype(BF16)
    sv_out[...] = proj[:, _O_SB + 2 * w:_O_QM].astype(BF16)

    for hd in range(MEM_HEADS):
        sl = slice(hd * MEM_DIM, (hd + 1) * MEM_DIM)
        vsl = slice(_QM + hd * MEM_DIM, _QM + (hd + 1) * MEM_DIM)
        qm = (proj[:, _O_QM + hd * MEM_DIM:_O_QM + (hd + 1) * MEM_DIM] * (MEM_DIM ** -0.5)).astype(BF16)
        s = _dot_t(qm, mkv_ref[:, sl])
        p = jnp.exp(s - jnp.max(s, axis=-1, keepdims=True))
        o = _dot(p.astype(BF16), mkv_ref[:, vsl])
        om_out[:, sl] = (o / jnp.sum(p, axis=-1, keepdims=True)).astype(BF16)


def _in_proj(x2d, pos2d, freq, g, w1, qn, wuq, wuqr, kvn, wuk, wuv, ones, mkv, seq):
    rows, d = x2d.shape
    tile = IN_TILE
    per_seq = seq // tile
    mem_len = mkv.shape[1]
    row = lambda c: pl.BlockSpec((tile, c), lambda i: (i, 0))
    widths = (MLA_HEADS * LANES, MLA_HEADS * LANES, MLA_HEADS * LANES, SB_HEADS * SB_DIM, SB_HEADS * SB_DIM,
              SB_HEADS * SB_DIM, MEM_HEADS * MEM_DIM)
    mem_spec = pl.BlockSpec((None, mem_len, 2 * _QM), lambda i: (i // per_seq, 0, 0))
    return pl.pallas_call(
        _in_proj_kernel,
        grid=(rows // tile,),
        in_specs=[row(d), pl.BlockSpec((tile * MLA_ROPE // LANES, LANES), lambda i: (i, 0)), _const_spec(freq.shape), _const_spec(g.shape), _const_spec(w1.shape),
                  _const_spec(qn.shape), _const_spec(wuq.shape), _const_spec(wuqr.shape), _const_spec(kvn.shape),
                  _const_spec(wuk.shape), _const_spec(wuv.shape), _const_spec(ones.shape), mem_spec],
        out_specs=[row(c) for c in widths],
        out_shape=[jax.ShapeDtypeStruct((rows, c), BF16) for c in widths],
        compiler_params=pltpu.CompilerParams(dimension_semantics=("parallel",), vmem_limit_bytes=VMEM_LIMIT),
        name="in_proj",
    )(x2d, pos2d, freq, g, w1, qn, wuq, wuqr, kvn, wuk, wuv, ones, mkv)


def _key_rows(ref, j, tile):
    if isinstance(j, int):
        return ref[j * tile:(j + 1) * tile, :]
    return ref[pl.ds(pl.multiple_of(j * tile, tile), tile), :]


def _cat_rows(parts):
    return jnp.concatenate(parts, axis=0)


def _lane_fold_max(x):
    top = x[:, :LANES]
    for c0 in range(LANES, x.shape[1], LANES):
        top = jnp.maximum(top, x[:, c0:c0 + LANES])
    return jnp.max(top, axis=-1, keepdims=True)


def _mla_kernel(q_ref, k_ref, v_ref, o_ref):
    i = pl.program_id(2)
    heads = [slice(hh * LANES, (hh + 1) * LANES) for hh in range(q_ref.shape[1] // LANES)]

    def tile(j, carry, diag):
        ms, accs = carry
        kb = _key_rows(k_ref, j, Q_TILE)
        vb = _key_rows(v_ref, j, Q_TILE)
        scores = [_dot_t(q_ref[:, hs], kb[:, hs]) for hs in heads]
        if diag:
            r = lax.broadcasted_iota(jnp.int32, (Q_TILE, Q_TILE), 0)
            c = lax.broadcasted_iota(jnp.int32, (Q_TILE, Q_TILE), 1)
            scores = [jnp.where(r >= c, s, NEG_BIG) for s in scores]
        new_m, new_acc = [], []
        for hh, hs in enumerate(heads):
            m = jnp.maximum(ms[hh], _lane_fold_max(scores[hh]))
            p = jnp.exp2(scores[hh] - m).astype(BF16)
            new_m.append(m)
            new_acc.append(jnp.exp2(ms[hh] - m) * accs[hh] + _dot(p, vb[:, hs]))
        return tuple(new_m), tuple(new_acc)

    def run(n_full):
        col = jnp.full((Q_TILE, 1), NEG_BIG, F32)
        zero = jnp.zeros((Q_TILE, LANES), F32)
        carry = ((col,) * len(heads), (zero,) * len(heads))
        for j in range(n_full):
            carry = tile(j, carry, False)
        _, accs = tile(n_full, carry, True)
        first = lax.broadcasted_iota(jnp.int32, (Q_TILE, LANES), 1) < MLA_V
        half_turn = lambda x: pltpu.roll(x, MLA_V, 1)
        for pr in range(len(heads) // 2):
            even, odd = accs[2 * pr], accs[2 * pr + 1]
            o_ref[:, heads[pr]] = jnp.where(first, even / half_turn(even), half_turn(odd) / odd).astype(BF16)

    for n_full in range(k_ref.shape[0] // Q_TILE):
        pl.when(i == n_full)(functools.partial(run, n_full))


def _sb_kernel(q_ref, k_ref, v_ref, o_ref):
    i = pl.program_id(2)
    slots = [slice(sl * LANES, (sl + 1) * LANES) for sl in range(q_ref.shape[1] // LANES)]
    first = lax.broadcasted_iota(jnp.int32, (SB_TILE, LANES), 1) < SB_DIM
    r = lax.broadcasted_iota(jnp.int32, (2 * SB_TILE, SB_TILE), 0) & (SB_TILE - 1)
    c = lax.broadcasted_iota(jnp.int32, (2 * SB_TILE, SB_TILE), 1)
    strict = r > c
    later = jnp.where(strict[:SB_TILE], 1.0, 0.0).astype(BF16)
    q2 = []
    for sl in slots:
        qp = q_ref[:, sl]
        zq = jnp.zeros_like(qp)
        q2.append(_cat_rows([jnp.where(first, qp, zq), jnp.where(first, zq, qp)]))

    def tile(j, tots, accs, diag):
        kb = _key_rows(k_ref, j, SB_TILE)
        vb = _key_rows(v_ref, j, SB_TILE)
        logits = [_dot_t(q2[n], kb[:, sl]) for n, sl in enumerate(slots)]
        drops, keeps = [], []
        for z in logits:
            lg = jnp.log2(1.0 + jnp.exp2(-jnp.abs(z)))
            drop = lg - jnp.minimum(z, 0.0)
            keep = z + drop
            drops.append(drop)
            keeps.append(jnp.where(strict, keep, 0.0) if diag else keep)
        rights = [_dot(keep.astype(BF16), later) for keep in keeps]
        new_tots, new_accs = [], []
        for n, sl in enumerate(slots):
            w = jnp.exp2(-tots[n] - (drops[n] + rights[n]))
            if diag:
                w = jnp.where(strict, w, 0.0)
            new_tots.append(tots[n] + rights[n][:, 0:1] + keeps[n][:, 0:1])
            new_accs.append(accs[n] + _dot(w.astype(BF16), vb[:, sl]))
        return tuple(new_tots), tuple(new_accs)

    def any_live(tots):
        low = tots[0]
        for tot in tots[1:]:
            low = jnp.minimum(low, tot)
        return (jnp.min(low) < SB_DEAD_LOG2).astype(jnp.int32)

    def live(state):
        t, go, _, _ = state
        return jnp.logical_and(t < i, go > 0)

    def step(state):
        t, _, tots, accs = state
        tots, accs = tile(i - 1 - t, tots, accs, False)
        return t + 1, any_live(tots), tots, accs

    def run(n_older):
        zero_tot = jnp.zeros((2 * SB_TILE, 1), F32)
        zero_acc = jnp.zeros((2 * SB_TILE, LANES), F32)
        tots, accs = tile(i, (zero_tot,) * len(slots), (zero_acc,) * len(slots), True)
        for t in range(n_older):
            tots, accs = tile(i - 1 - t, tots, accs, False)
        if n_older:
            _, _, _, accs = lax.while_loop(live, step, (jnp.int32(n_older), any_live(tots), tots, accs))
        for n, sl in enumerate(slots):
            o_ref[:, sl] = jnp.where(first, accs[n][:SB_TILE], accs[n][SB_TILE:]).astype(BF16)

    pl.when(i == 0)(functools.partial(run, 0))
    pl.when(i > 0)(functools.partial(run, 1))


def _causal_mixer(body, name, tile, pairs, q, k, v, qw, kw, vw, ow):
    b, s, _ = q.shape
    groups = v.shape[2] // (vw * pairs)
    qw, kw, vw, ow = qw * pairs, kw * pairs, vw * pairs, ow * pairs
    return pl.pallas_call(
        body,
        grid=(b, groups, s // tile),
        in_specs=[pl.BlockSpec((None, tile, qw), lambda bi, g, i: (bi, i, g)),
                  pl.BlockSpec((None, s, kw), lambda bi, g, i: (bi, 0, g)),
                  pl.BlockSpec((None, s, vw), lambda bi, g, i: (bi, 0, g))],
        out_specs=pl.BlockSpec((None, tile, ow), lambda bi, g, i: (bi, i, g)),
        out_shape=jax.ShapeDtypeStruct((b, s, groups * ow), BF16),
        compiler_params=pltpu.CompilerParams(dimension_semantics=("parallel", "parallel", "arbitrary"),
                                             vmem_limit_bytes=VMEM_LIMIT),
        name=name,
    )(q, k, v)


def _post_kernel(x_ref, oa_ref, ob_ref, om_ref, g_pre_ref, wg_ref, bg_ref, wbo_ref, wout_ref, g_post_ref,
                 g_mlp_ref, wup_ref, wdn_ref, g_mlp_post_ref, o_ref):
    x = x_ref[...]
    d = x.shape[1]
    h = _rms(x, g_pre_ref[...]).astype(BF16)
    merged = None
    for br, o_br in enumerate((oa_ref, ob_ref, om_ref)):
        sl = slice(br * d, (br + 1) * d)
        gate = jax.nn.sigmoid(_dot(h, wg_ref[:, sl]) + bg_ref[:, sl])
        yb = gate * _dot(o_br[...], wbo_ref[br])
        merged = yb if merged is None else merged + yb
    y = _dot(merged.astype(BF16), wout_ref[...])
    x1 = x + _rms(y, g_post_ref[...])

    h2 = _rms(x1, g_mlp_ref[...]).astype(BF16)
    down = None
    for c0 in range(0, wup_ref.shape[1], FF_CHUNK):
        u = jnp.maximum(_dot(h2, wup_ref[:, c0:c0 + FF_CHUNK]), 0.0)
        part = _dot((u * u).astype(BF16), wdn_ref[c0:c0 + FF_CHUNK, :])
        down = part if down is None else down + part
    o_ref[...] = x1 + _rms(down, g_mlp_post_ref[...])


def _post(x2d, oa, ob, om, g_pre, wg, bg, wbo, wout, g_post, g_mlp, wup, wdn, g_mlp_post):
    rows, d = x2d.shape
    tile = POST_TILE
    row = lambda c: pl.BlockSpec((tile, c), lambda i: (i, 0))
    consts = (g_pre, wg, bg, wbo, wout, g_post, g_mlp, wup, wdn, g_mlp_post)
    return pl.pallas_call(
        _post_kernel,
        grid=(rows // tile,),
        in_specs=[row(d), row(oa.shape[1]), row(ob.shape[1]), row(om.shape[1])] + [_const_spec(a.shape) for a in consts],
        out_specs=row(d),
        out_shape=jax.ShapeDtypeStruct((rows, d), F32),
        compiler_params=pltpu.CompilerParams(dimension_semantics=("parallel",), vmem_limit_bytes=VMEM_LIMIT),
        name="post",
    )(x2d, oa, ob, om, *consts)


def _pad_heads(w, heads, width):
    r = w.shape[0]
    w = w.reshape(r, heads, width)
    return jnp.pad(w, ((0, 0), (0, 0), (0, LANES - width))).reshape(r, heads * LANES)


def _rotate_half_cols(w_rope):
    half = MLA_ROPE // 2
    return jnp.concatenate([-w_rope[..., half:], w_rope[..., :half]], axis=-1)


def _rope_slot(w_rope):
    r, heads, _ = w_rope.shape
    return jnp.pad(w_rope, ((0, 0), (0, 0), (MLA_NOPE, LANES - MLA_NOPE - MLA_ROPE))).reshape(r, heads * LANES)


def kernel(x, mem, positions, ln_mix_pre, w_in, b_gate, q_norm, w_uq, kv_norm, w_uk, w_uv, mem_norm, w_mem_kv,
           w_branch_out, w_out, ln_mix_post, ln_mlp_pre, w_mlp_up, w_mlp_down, ln_mlp_post):
    b, s, d = x.shape
    m = mem.shape[1]
    depth = w_in.shape[0]
    half = MLA_ROPE // 2
    inv_freq = 1.0 / (ROPE_THETA ** (jnp.arange(half, dtype=F32) * (2.0 / MLA_ROPE)))
    groups = LANES // MLA_ROPE
    freq = jnp.tile(jnp.concatenate([inv_freq, inv_freq]), groups).reshape(1, LANES)
    ones = jnp.tile((jnp.arange(LANES) >= MLA_V).astype(F32), MLA_HEADS).reshape(1, -1)
    pos2d = positions.astype(F32).reshape(b * s // IN_TILE, groups, IN_TILE // groups).swapaxes(1, 2)
    pos2d = jnp.repeat(pos2d, MLA_ROPE, axis=2).reshape(b * s // groups, LANES)
    mem2d = mem.reshape(b * m, d)
    x2d = x.reshape(b * s, d)
    row = lambda a: a.reshape(1, -1)

    for l in range(depth):
        cq_w, ckv_w, kr_w, sb_w, qm_w, gate_w = jnp.split(
            w_in[l].astype(BF16), np.cumsum([_CQ, _CKV, MLA_ROPE, _SB, _QM]).tolist(), axis=1)
        kr3 = kr_w.reshape(d, 1, MLA_ROPE)
        w1 = jnp.concatenate([cq_w, ckv_w, sb_w, qm_w, _rope_slot(kr3), _rope_slot(_rotate_half_cols(kr3)),
                              jnp.zeros((d, _W1_PAD), BF16)], axis=1)
        uq = w_uq[l].reshape(-1, MLA_HEADS, MLA_NOPE + MLA_ROPE)
        wuq = _pad_heads(w_uq[l], MLA_HEADS, MLA_NOPE + MLA_ROPE).astype(BF16)
        wuqr = _rope_slot(_rotate_half_cols(uq[..., MLA_NOPE:])).astype(BF16)
        wuk = _pad_heads(w_uk[l], MLA_HEADS, MLA_NOPE).astype(BF16)

        mkv = _mem_kv(mem2d, row(mem_norm[l]), w_mem_kv[l].astype(BF16))
        mkv = mkv.reshape(b, m, 2 * MEM_HEADS * MEM_DIM)

        q, k, v, sq, sk, sv, o_mem = _in_proj(
            x2d, pos2d, freq, row(ln_mix_pre[l]), w1, row(q_norm[l]), wuq, wuqr, row(kv_norm[l]), wuk,
            _pad_heads(w_uv[l], MLA_HEADS, MLA_V).astype(BF16), ones, mkv, s)

        seq = lambda a: a.reshape(b, s, -1)
        o_mla = _causal_mixer(_mla_kernel, "mla_attn", Q_TILE, MLA_PAIRS, seq(q), seq(k), seq(v),
                              2 * LANES, 2 * LANES, 2 * LANES, LANES)
        o_sb = _causal_mixer(_sb_kernel, "sb_attn", SB_TILE, SB_PAIRS, seq(sq), seq(sk), seq(sv),
                             LANES, LANES, LANES, LANES)

        x2d = _post(x2d, o_mla.reshape(b * s, -1), o_sb.reshape(b * s, -1), o_mem, row(ln_mix_pre[l]),
                    gate_w, row(b_gate[l]), w_branch_out[l].astype(BF16), w_out[l].astype(BF16),
                    row(ln_mix_post[l]), row(ln_mlp_pre[l]), w_mlp_up[l].astype(BF16), w_mlp_down[l].astype(BF16),
                    row(ln_mlp_post[l]))
    return x2d.reshape(b, s, d)
```

```python
import functools

import jax
import jax.numpy as jnp
import numpy as np
from jax import lax
from jax.experimental import pallas as pl
from jax.experimental.pallas import tpu as pltpu

F32 = jnp.float32
BF16 = jnp.bfloat16

LANES = 128
EPS = 1e-6
ROPE_THETA = 10000.0

MLA_HEADS, MLA_NOPE, MLA_ROPE, MLA_V = 8, 64, 32, 64
SB_HEADS, SB_DIM = 8, 64
MEM_HEADS, MEM_DIM = 4, 128
N_BRANCH = 3
NEG_BIG = -1e30

IN_TILE, POST_TILE = 1024, 512
Q_TILE = 512
SB_TILE = 256
MLA_PAIRS, SB_PAIRS = 2, 4
SB_DEAD_LOG2 = 160.0
LOG2E = 1.4426950408889634
FF_CHUNK = 1024
VMEM_LIMIT = 60 * 1024 * 1024


def _rms(x, g):
    return x * lax.rsqrt(jnp.mean(x * x, axis=-1, keepdims=True) + EPS) * g


def _dot(a, b):
    return jnp.dot(a, b, preferred_element_type=F32)


def _dot_t(a, b):
    return lax.dot_general(a, b, (((1,), (1,)), ((), ())), preferred_element_type=F32)


def _const_spec(shape):
    nd = len(shape)
    return pl.BlockSpec(shape, lambda *_: (0,) * nd, pipeline_mode=pl.Buffered(1))


def _mem_kv_kernel(mem_ref, g_ref, w_ref, o_ref):
    h = _rms(mem_ref[...], g_ref[...]).astype(BF16)
    o_ref[...] = _dot(h, w_ref[...]).astype(BF16)


def _mem_kv(mem2d, g, w):
    rows, d = mem2d.shape
    n = w.shape[1]
    tile = 512
    return pl.pallas_call(
        _mem_kv_kernel,
        grid=(rows // tile,),
        in_specs=[pl.BlockSpec((tile, d), lambda i: (i, 0)), _const_spec((1, d)), _const_spec((d, n))],
        out_specs=pl.BlockSpec((tile, n), lambda i: (i, 0)),
        out_shape=jax.ShapeDtypeStruct((rows, n), BF16),
        compiler_params=pltpu.CompilerParams(dimension_semantics=("parallel",), vmem_limit_bytes=VMEM_LIMIT),
        name="mem_kv",
    )(mem2d, g, w)


_CQ, _CKV, _SB, _QM, _KR = 384, 256, 3 * SB_HEADS * SB_DIM, MEM_HEADS * MEM_DIM, LANES
_O_CKV = _CQ
_O_SB = _O_CKV + _CKV
_O_QM = _O_SB + _SB
_O_KR = _O_QM + _QM
_W1 = _O_KR + _KR
MXU_WIDTH = 256
_W1_PAD = -_W1 % MXU_WIDTH


def _in_proj_kernel(x_ref, pos_ref, freq_ref, g_ref, w1_ref, qn_ref, wuq_ref, wuqr_ref, kvn_ref, wuk_ref,
                    wuv_ref, ones_ref, mkv_ref, q_out, k_out, v_out, sq_out, sk_out, sv_out, om_out):
    groups = LANES // MLA_ROPE
    blk = pos_ref.shape[0]
    ang = pos_ref[...] * freq_ref[...]
    cos_packed, sin_packed = jnp.cos(ang), jnp.sin(ang)
    lane = lax.broadcasted_iota(jnp.int32, (blk, LANES), 1)
    rotary = jnp.logical_and(lane >= MLA_NOPE, lane < MLA_NOPE + MLA_ROPE)
    cos_blocks, sin_blocks = [], []
    for gi in range(groups):
        shift = (MLA_NOPE - gi * MLA_ROPE) % LANES
        place = (lambda t: t) if shift == 0 else (lambda t: pltpu.roll(t, shift, 1))
        cos_blocks.append(jnp.where(rotary, place(cos_packed), 1.0))
        sin_blocks.append(jnp.where(rotary, place(sin_packed), 0.0))
    cos, sin = jnp.concatenate(cos_blocks, axis=0), jnp.concatenate(sin_blocks, axis=0)

    h = _rms(x_ref[...], g_ref[...]).astype(BF16)
    proj = _dot(h, w1_ref[...])

    cq = _rms(proj[:, :_CQ], qn_ref[...]).astype(BF16)
    q = _dot(cq, wuq_ref[...])
    qr = _dot(cq, wuqr_ref[...])
    ckv = _rms(proj[:, _O_CKV:_O_SB], kvn_ref[...]).astype(BF16)
    kn = _dot(ckv, wuk_ref[...])
    v_out[...] = (_dot(ckv, wuv_ref[...]) + ones_ref[...]).astype(BF16)
    kr_pair = proj[:, _O_KR:_W1]
    lane_full = lax.broadcasted_iota(jnp.int32, kr_pair.shape, 1)
    on_rotary = jnp.logical_and(lane_full >= MLA_NOPE, lane_full < MLA_NOPE + MLA_ROPE)
    kr = jnp.where(on_rotary, kr_pair * cos + pltpu.roll(kr_pair, MLA_NOPE, 1) * sin, 0.0)
    q_scale = (MLA_NOPE + MLA_ROPE) ** -0.5 * LOG2E
    for hd in range(MLA_HEADS):
        sl = slice(hd * LANES, (hd + 1) * LANES)
        at = hd * MLA_ROPE
        qr_slot = qr[:, at // LANES * LANES:(at // LANES + 1) * LANES]
        shift = (MLA_NOPE - at % LANES) % LANES
        qr_head = qr_slot if shift == 0 else pltpu.roll(qr_slot, shift, 1)
        q_out[:, sl] = ((q[:, sl] * cos + qr_head * sin) * q_scale).astype(BF16)
        k_out[:, sl] = (kn[:, sl] + kr).astype(BF16)

    w = SB_HEADS * SB_DIM
    sq_out[...] = (proj[:, _O_SB:_O_SB + w] * (SB_DIM ** -0.5 * LOG2E)).astype(BF16)
    sk_out[...] = proj[:, _O_SB + w:_O_SB + 2 * w].astype(BF16)
    sv_out[...] = proj[:, _O_SB + 2 * w:_O_QM].astype(BF16)

    for hd in range(MEM_HEADS):
        sl = slice(hd * MEM_DIM, (hd + 1) * MEM_DIM)
        vsl = slice(_QM + hd * MEM_DIM, _QM + (hd + 1) * MEM_DIM)
        qm = (proj[:, _O_QM + hd * MEM_DIM:_O_QM + (hd + 1) * MEM_DIM] * (MEM_DIM ** -0.5)).astype(BF16)
        s = _dot_t(qm, mkv_ref[:, sl])
        p = jnp.exp(s - jnp.max(s, axis=-1, keepdims=True))
        o = _dot(p.astype(BF16), mkv_ref[:, vsl])
        om_out[:, sl] = (o / jnp.sum(p, axis=-1, keepdims=True)).astype(BF16)


def _in_proj(x2d, pos2d, freq, g, w1, qn, wuq, wuqr, kvn, wuk, wuv, ones, mkv, seq):
    rows, d = x2d.shape
    tile = IN_TILE
    per_seq = seq // tile
    mem_len = mkv.shape[1]
    row = lambda c: pl.BlockSpec((tile, c), lambda i: (i, 0))
    widths = (MLA_HEADS * LANES, MLA_HEADS * LANES, MLA_HEADS * LANES, SB_HEADS * SB_DIM, SB_HEADS * SB_DIM,
              SB_HEADS * SB_DIM, MEM_HEADS * MEM_DIM)
    mem_spec = pl.BlockSpec((None, mem_len, 2 * _QM), lambda i: (i // per_seq, 0, 0))
    return pl.pallas_call(
        _in_proj_kernel,
        grid=(rows // tile,),
        in_specs=[row(d), pl.BlockSpec((tile * MLA_ROPE // LANES, LANES), lambda i: (i, 0)), _const_spec(freq.shape), _const_spec(g.shape), _const_spec(w1.shape),
                  _const_spec(qn.shape), _const_spec(wuq.shape), _const_spec(wuqr.shape), _const_spec(kvn.shape),
                  _const_spec(wuk.shape), _const_spec(wuv.shape), _const_spec(ones.shape), mem_spec],
        out_specs=[row(c) for c in widths],
        out_shape=[jax.ShapeDtypeStruct((rows, c), BF16) for c in widths],
        compiler_params=pltpu.CompilerParams(dimension_semantics=("parallel",), vmem_limit_bytes=VMEM_LIMIT),
        name="in_proj",
    )(x2d, pos2d, freq, g, w1, qn, wuq, wuqr, kvn, wuk, wuv, ones, mkv)


def _key_rows(ref, j, tile):
    if isinstance(j, int):
        return ref[j * tile:(j + 1) * tile, :]
    return ref[pl.ds(pl.multiple_of(j * tile, tile), tile), :]


def _cat_rows(parts):
    return jnp.concatenate(parts, axis=0)


def _lane_fold_max(x):
    top = x[:, :LANES]
    for c0 in range(LANES, x.shape[1], LANES):
        top = jnp.maximum(top, x[:, c0:c0 + LANES])
    return jnp.max(top, axis=-1, keepdims=True)


def _mla_kernel(q_ref, k_ref, v_ref, o_ref):
    i = pl.program_id(2)
    heads = [slice(hh * LANES, (hh + 1) * LANES) for hh in range(q_ref.shape[1] // LANES)]

    def tile(j, carry, diag):
        ms, accs = carry
        kb = _key_rows(k_ref, j, Q_TILE)
        vb = _key_rows(v_ref, j, Q_TILE)
        scores = [_dot_t(q_ref[:, hs], kb[:, hs]) for hs in heads]
        if diag:
            r = lax.broadcasted_iota(jnp.int32, (Q_TILE, Q_TILE), 0)
            c = lax.broadcasted_iota(jnp.int32, (Q_TILE, Q_TILE), 1)
            scores = [jnp.where(r >= c, s, NEG_BIG) for s in scores]
        new_m, new_acc = [], []
        for hh, hs in enumerate(heads):
            m = jnp.maximum(ms[hh], _lane_fold_max(scores[hh]))
            p = jnp.exp2(scores[hh] - m).astype(BF16)
            new_m.append(m)
            new_acc.append(jnp.exp2(ms[hh] - m) * accs[hh] + _dot(p, vb[:, hs]))
        return tuple(new_m), tuple(new_acc)

    def run(n_full):
        col = jnp.full((Q_TILE, 1), NEG_BIG, F32)
        zero = jnp.zeros((Q_TILE, LANES), F32)
        carry = ((col,) * len(heads), (zero,) * len(heads))
        for j in range(n_full):
            carry = tile(j, carry, False)
        _, accs = tile(n_full, carry, True)
        first = lax.broadcasted_iota(jnp.int32, (Q_TILE, LANES), 1) < MLA_V
        half_turn = lambda x: pltpu.roll(x, MLA_V, 1)
        for pr in range(len(heads) // 2):
            even, odd = accs[2 * pr], accs[2 * pr + 1]
            o_ref[:, heads[pr]] = jnp.where(first, even / half_turn(even), half_turn(odd) / odd).astype(BF16)

    for n_full in range(k_ref.shape[0] // Q_TILE):
        pl.when(i == n_full)(functools.partial(run, n_full))


def _sb_kernel(q_ref, k_ref, v_ref, o_ref):
    i = pl.program_id(2)
    slots = [slice(sl * LANES, (sl + 1) * LANES) for sl in range(q_ref.shape[1] // LANES)]
    first = lax.broadcasted_iota(jnp.int32, (SB_TILE, LANES), 1) < SB_DIM
    r = lax.broadcasted_iota(jnp.int32, (2 * SB_TILE, SB_TILE), 0) & (SB_TILE - 1)
    c = lax.broadcasted_iota(jnp.int32, (2 * SB_TILE, SB_TILE), 1)
    strict = r > c
    later = jnp.where(strict[:SB_TILE], 1.0, 0.0).astype(BF16)
    q2 = []
    for sl in slots:
        qp = q_ref[:, sl]
        zq = jnp.zeros_like(qp)
        q2.append(_cat_rows([jnp.where(first, qp, zq), jnp.where(first, zq, qp)]))

    def tile(j, tots, accs, diag):
        kb = _key_rows(k_ref, j, SB_TILE)
        vb = _key_rows(v_ref, j, SB_TILE)
        logits = [_dot_t(q2[n], kb[:, sl]) for n, sl in enumerate(slots)]
        drops, keeps = [], []
        for z in logits:
            lg = jnp.log2(1.0 + jnp.exp2(-jnp.abs(z)))
            drop = lg - jnp.minimum(z, 0.0)
            keep = z + drop
            drops.append(drop)
            keeps.append(jnp.where(strict, keep, 0.0) if diag else keep)
        rights = [_dot(keep.astype(BF16), later) for keep in keeps]
        new_tots, new_accs = [], []
        for n, sl in enumerate(slots):
            w = jnp.exp2(-tots[n] - (drops[n] + rights[n]))
            if diag:
                w = jnp.where(strict, w, 0.0)
            new_tots.append(tots[n] + rights[n][:, 0:1] + keeps[n][:, 0:1])
            new_accs.append(accs[n] + _dot(w.astype(BF16), vb[:, sl]))
        return tuple(new_tots), tuple(new_accs)

    def any_live(tots):
        low = tots[0]
        for tot in tots[1:]:
            low = jnp.minimum(low, tot)
        return (jnp.min(low) < SB_DEAD_LOG2).astype(jnp.int32)

    def live(state):
        t, go, _, _ = state
        return jnp.logical_and(t < i, go > 0)

    def step(state):
        t, _, tots, accs = state
        tots, accs = tile(i - 1 - t, tots, accs, False)
        return t + 1, any_live(tots), tots, accs

    def run(n_older):
        zero_tot = jnp.zeros((2 * SB_TILE, 1), F32)
        zero_acc = jnp.zeros((2 * SB_TILE, LANES), F32)
        tots, accs = tile(i, (zero_tot,) * len(slots), (zero_acc,) * len(slots), True)
        for t in range(n_older):
            tots, accs = tile(i - 1 - t, tots, accs, False)
        if n_older:
            _, _, _, accs = lax.while_loop(live, step, (jnp.int32(n_older), any_live(tots), tots, accs))
        for n, sl in enumerate(slots):
            o_ref[:, sl] = jnp.where(first, accs[n][:SB_TILE], accs[n][SB_TILE:]).astype(BF16)

    pl.when(i == 0)(functools.partial(run, 0))
    pl.when(i > 0)(functools.partial(run, 1))


def _causal_mixer(body, name, tile, pairs, q, k, v, qw, kw, vw, ow):
    b, s, _ = q.shape
    groups = v.shape[2] // (vw * pairs)
    qw, kw, vw, ow = qw * pairs, kw * pairs, vw * pairs, ow * pairs
    return pl.pallas_call(
        body,
        grid=(b, groups, s // tile),
        in_specs=[pl.BlockSpec((None, tile, qw), lambda bi, g, i: (bi, i, g)),
                  pl.BlockSpec((None, s, kw), lambda bi, g, i: (bi, 0, g)),
                  pl.BlockSpec((None, s, vw), lambda bi, g, i: (bi, 0, g))],
        out_specs=pl.BlockSpec((None, tile, ow), lambda bi, g, i: (bi, i, g)),
        out_shape=jax.ShapeDtypeStruct((b, s, groups * ow), BF16),
        compiler_params=pltpu.CompilerParams(dimension_semantics=("parallel", "parallel", "arbitrary"),
                                             vmem_limit_bytes=VMEM_LIMIT),
        name=name,
    )(q, k, v)


def _post_kernel(x_ref, oa_ref, ob_ref, om_ref, g_pre_ref, wg_ref, bg_ref, wbo_ref, wout_ref, g_post_ref,
                 g_mlp_ref, wup_ref, wdn_ref, g_mlp_post_ref, o_ref):
    x = x_ref[...]
    d = x.shape[1]
    h = _rms(x, g_pre_ref[...]).astype(BF16)
    merged = None
    for br, o_br in enumerate((oa_ref, ob_ref, om_ref)):
        sl = slice(br * d, (br + 1) * d)
        gate = jax.nn.sigmoid(_dot(h, wg_ref[:, sl]) + bg_ref[:, sl])
        yb = gate * _dot(o_br[...], wbo_ref[br])
        merged = yb if merged is None else merged + yb
    y = _dot(merged.astype(BF16), wout_ref[...])
    x1 = x + _rms(y, g_post_ref[...])

    h2 = _rms(x1, g_mlp_ref[...]).astype(BF16)
    down = None
    for c0 in range(0, wup_ref.shape[1], FF_CHUNK):
        u = jnp.maximum(_dot(h2, wup_ref[:, c0:c0 + FF_CHUNK]), 0.0)
        part = _dot((u * u).astype(BF16), wdn_ref[c0:c0 + FF_CHUNK, :])
        down = part if down is None else down + part
    o_ref[...] = x1 + _rms(down, g_mlp_post_ref[...])


def _post(x2d, oa, ob, om, g_pre, wg, bg, wbo, wout, g_post, g_mlp, wup, wdn, g_mlp_post):
    rows, d = x2d.shape
    tile = POST_TILE
    row = lambda c: pl.BlockSpec((tile, c), lambda i: (i, 0))
    consts = (g_pre, wg, bg, wbo, wout, g_post, g_mlp, wup, wdn, g_mlp_post)
    return pl.pallas_call(
        _post_kernel,
        grid=(rows // tile,),
        in_specs=[row(d), row(oa.shape[1]), row(ob.shape[1]), row(om.shape[1])] + [_const_spec(a.shape) for a in consts],
        out_specs=row(d),
        out_shape=jax.ShapeDtypeStruct((rows, d), F32),
        compiler_params=pltpu.CompilerParams(dimension_semantics=("parallel",), vmem_limit_bytes=VMEM_LIMIT),
        name="post",
    )(x2d, oa, ob, om, *consts)


def _pad_heads(w, heads, width):
    r = w.shape[0]
    w = w.reshape(r, heads, width)
    return jnp.pad(w, ((0, 0), (0, 0), (0, LANES - width))).reshape(r, heads * LANES)


def _rotate_half_cols(w_rope):
    half = MLA_ROPE // 2
    return jnp.concatenate([-w_rope[..., half:], w_rope[..., :half]], axis=-1)


def kernel(x, mem, positions, ln_mix_pre, w_in, b_gate, q_norm, w_uq, kv_norm, w_uk, w_uv, mem_norm, w_mem_kv,
           w_branch_out, w_out, ln_mix_post, ln_mlp_pre, w_mlp_up, w_mlp_down, ln_mlp_post):
    b, s, d = x.shape
    m = mem.shape[1]
    depth = w_in.shape[0]
    half = MLA_ROPE // 2
    inv_freq = 1.0 / (ROPE_THETA ** (jnp.arange(half, dtype=F32) * (2.0 / MLA_ROPE)))
    groups = LANES // MLA_ROPE
    freq = jnp.tile(jnp.concatenate([inv_freq, inv_freq]), groups).reshape(1, LANES)
    ones = jnp.tile((jnp.arange(LANES) >= MLA_V).astype(F32), MLA_HEADS).reshape(1, -1)
    pos2d = positions.astype(F32).reshape(b * s // IN_TILE, groups, IN_TILE // groups).swapaxes(1, 2)
    pos2d = jnp.repeat(pos2d, MLA_ROPE, axis=2).reshape(b * s // groups, LANES)
    mem2d = mem.reshape(b * m, d)
    x2d = x.reshape(b * s, d)
    row = lambda a: a.reshape(1, -1)

    for l in range(depth):
        cq_w, ckv_w, kr_w, sb_w, qm_w, gate_w = jnp.split(
            w_in[l].astype(BF16), np.cumsum([_CQ, _CKV, MLA_ROPE, _SB, _QM]).tolist(), axis=1)
        gap = jnp.zeros((d, MLA_ROPE), BF16)
        w1 = jnp.concatenate([cq_w, ckv_w, sb_w, qm_w, _rotate_half_cols(kr_w), gap, kr_w, gap,
                              jnp.zeros((d, _W1_PAD), BF16)], axis=1)
        uq = w_uq[l].reshape(-1, MLA_HEADS, MLA_NOPE + MLA_ROPE)
        wuq = _pad_heads(w_uq[l], MLA_HEADS, MLA_NOPE + MLA_ROPE).astype(BF16)
        wuqr = _rotate_half_cols(uq[..., MLA_NOPE:]).reshape(-1, MLA_HEADS * MLA_ROPE).astype(BF16)
        wuk = _pad_heads(w_uk[l], MLA_HEADS, MLA_NOPE).astype(BF16)

        mkv = _mem_kv(mem2d, row(mem_norm[l]), w_mem_kv[l].astype(BF16))
        mkv = mkv.reshape(b, m, 2 * MEM_HEADS * MEM_DIM)

        q, k, v, sq, sk, sv, o_mem = _in_proj(
            x2d, pos2d, freq, row(ln_mix_pre[l]), w1, row(q_norm[l]), wuq, wuqr, row(kv_norm[l]), wuk,
            _pad_heads(w_uv[l], MLA_HEADS, MLA_V).astype(BF16), ones, mkv, s)

        seq = lambda a: a.reshape(b, s, -1)
        o_mla = _causal_mixer(_mla_kernel, "mla_attn", Q_TILE, MLA_PAIRS, seq(q), seq(k), seq(v),
                              2 * LANES, 2 * LANES, 2 * LANES, LANES)
        o_sb = _causal_mixer(_sb_kernel, "sb_attn", SB_TILE, SB_PAIRS, seq(sq), seq(sk), seq(sv),
                             LANES, LANES, LANES, LANES)

        x2d = _post(x2d, o_mla.reshape(b * s, -1), o_sb.reshape(b * s, -1), o_mem, row(ln_mix_pre[l]),
                    gate_w, row(b_gate[l]), w_branch_out[l].astype(BF16), w_out[l].astype(BF16),
                    row(ln_mix_post[l]), row(ln_mlp_pre[l]), w_mlp_up[l].astype(BF16), w_mlp_down[l].astype(BF16),
                    row(ln_mlp_post[l]))
    return x2d.reshape(b, s, d)
```

```python
import functools

import jax
import jax.numpy as jnp
import numpy as np
from jax import lax
from jax.experimental import pallas as pl
from jax.experimental.pallas import tpu as pltpu

F32 = jnp.float32
BF16 = jnp.bfloat16

LANES = 128
EPS = 1e-6
ROPE_THETA = 10000.0

MLA_HEADS, MLA_NOPE, MLA_ROPE, MLA_V = 8, 64, 32, 64
SB_HEADS, SB_DIM = 8, 64
MEM_HEADS, MEM_DIM = 4, 128
N_BRANCH = 3
NEG_BIG = -1e30

IN_TILE, POST_TILE = 1024, 512
Q_TILE = 512
SB_TILE = 256
MLA_PAIRS, SB_PAIRS = 2, 4
SB_DEAD_LOG2 = 160.0
LOG2E = 1.4426950408889634
FF_CHUNK = 1024
VMEM_LIMIT = 60 * 1024 * 1024


def _rms(x, g):
    return x * lax.rsqrt(jnp.mean(x * x, axis=-1, keepdims=True) + EPS) * g


def _dot(a, b):
    return jnp.dot(a, b, preferred_element_type=F32)


def _dot_t(a, b):
    return lax.dot_general(a, b, (((1,), (1,)), ((), ())), preferred_element_type=F32)


def _const_spec(shape):
    nd = len(shape)
    return pl.BlockSpec(shape, lambda *_: (0,) * nd, pipeline_mode=pl.Buffered(1))


_CQ, _CKV, _SB, _QM, _KR = 384, 256, 3 * SB_HEADS * SB_DIM, MEM_HEADS * MEM_DIM, LANES
_O_CKV = _CQ
_O_SB = _O_CKV + _CKV
_O_QM = _O_SB + _SB
_O_KR = _O_QM + _QM
_W1 = _O_KR + _KR
MXU_WIDTH = 256
_W1_PAD = -_W1 % MXU_WIDTH


def _in_proj_kernel(x_ref, pos_ref, freq_ref, g_ref, w1_ref, qn_ref, wuq_ref, wuqr_ref, kvn_ref, wuk_ref,
                    wuv_ref, ones_ref, mem_ref, gm_ref, wm_ref, q_out, k_out, v_out, sq_out, sk_out, sv_out, om_out,
                    mkv_ref, *, steps_per_seq):
    @pl.when(pl.program_id(0) % steps_per_seq == 0)
    def _():
        mem_h = _rms(mem_ref[...], gm_ref[...]).astype(BF16)
        mkv_ref[...] = _dot(mem_h, wm_ref[...]).astype(BF16)

    groups = LANES // MLA_ROPE
    blk = pos_ref.shape[0]
    ang = pos_ref[...] * freq_ref[...]
    cos_packed, sin_packed = jnp.cos(ang), jnp.sin(ang)
    lane = lax.broadcasted_iota(jnp.int32, (blk, LANES), 1)
    rotary = jnp.logical_and(lane >= MLA_NOPE, lane < MLA_NOPE + MLA_ROPE)
    cos_blocks, sin_blocks = [], []
    for gi in range(groups):
        shift = (MLA_NOPE - gi * MLA_ROPE) % LANES
        place = (lambda t: t) if shift == 0 else (lambda t: pltpu.roll(t, shift, 1))
        cos_blocks.append(jnp.where(rotary, place(cos_packed), 1.0))
        sin_blocks.append(jnp.where(rotary, place(sin_packed), 0.0))
    cos, sin = jnp.concatenate(cos_blocks, axis=0), jnp.concatenate(sin_blocks, axis=0)

    h = _rms(x_ref[...], g_ref[...]).astype(BF16)
    proj = _dot(h, w1_ref[...])

    cq = _rms(proj[:, :_CQ], qn_ref[...]).astype(BF16)
    q = _dot(cq, wuq_ref[...])
    qr = _dot(cq, wuqr_ref[...])
    ckv = _rms(proj[:, _O_CKV:_O_SB], kvn_ref[...]).astype(BF16)
    kn = _dot(ckv, wuk_ref[...])
    v_out[...] = (_dot(ckv, wuv_ref[...]) + ones_ref[...]).astype(BF16)
    kr_pair = proj[:, _O_KR:_W1]
    lane_full = lax.broadcasted_iota(jnp.int32, kr_pair.shape, 1)
    on_rotary = jnp.logical_and(lane_full >= MLA_NOPE, lane_full < MLA_NOPE + MLA_ROPE)
    kr = jnp.where(on_rotary, kr_pair * cos + pltpu.roll(kr_pair, MLA_NOPE, 1) * sin, 0.0)
    q_scale = (MLA_NOPE + MLA_ROPE) ** -0.5 * LOG2E
    for hd in range(MLA_HEADS):
        sl = slice(hd * LANES, (hd + 1) * LANES)
        at = hd * MLA_ROPE
        qr_slot = qr[:, at // LANES * LANES:(at // LANES + 1) * LANES]
        shift = (MLA_NOPE - at % LANES) % LANES
        qr_head = qr_slot if shift == 0 else pltpu.roll(qr_slot, shift, 1)
        q_out[:, sl] = ((q[:, sl] * cos + qr_head * sin) * q_scale).astype(BF16)
        k_out[:, sl] = (kn[:, sl] + kr).astype(BF16)

    w = SB_HEADS * SB_DIM
    sq_out[...] = (proj[:, _O_SB:_O_SB + w] * (SB_DIM ** -0.5 * LOG2E)).astype(BF16)
    sk_out[...] = proj[:, _O_SB + w:_O_SB + 2 * w].astype(BF16)
    sv_out[...] = proj[:, _O_SB + 2 * w:_O_QM].astype(BF16)

    for hd in range(MEM_HEADS):
        sl = slice(hd * MEM_DIM, (hd + 1) * MEM_DIM)
        vsl = slice(_QM + hd * MEM_DIM, _QM + (hd + 1) * MEM_DIM)
        qm = (proj[:, _O_QM + hd * MEM_DIM:_O_QM + (hd + 1) * MEM_DIM] * (MEM_DIM ** -0.5)).astype(BF16)
        s = _dot_t(qm, mkv_ref[:, sl])
        p = jnp.exp(s - jnp.max(s, axis=-1, keepdims=True))
        o = _dot(p.astype(BF16), mkv_ref[:, vsl])
        om_out[:, sl] = (o / jnp.sum(p, axis=-1, keepdims=True)).astype(BF16)


def _in_proj(x2d, pos2d, freq, g, w1, qn, wuq, wuqr, kvn, wuk, wuv, ones, mem, gm, wm, seq):
    rows, d = x2d.shape
    tile = IN_TILE
    per_seq = seq // tile
    mem_len = mem.shape[1]
    row = lambda c: pl.BlockSpec((tile, c), lambda i: (i, 0))
    widths = (MLA_HEADS * LANES, MLA_HEADS * LANES, MLA_HEADS * LANES, SB_HEADS * SB_DIM, SB_HEADS * SB_DIM,
              SB_HEADS * SB_DIM, MEM_HEADS * MEM_DIM)
    mem_spec = pl.BlockSpec((None, mem_len, d), lambda i: (i // per_seq, 0, 0))
    return pl.pallas_call(
        functools.partial(_in_proj_kernel, steps_per_seq=per_seq),
        grid=(rows // tile,),
        in_specs=[row(d), pl.BlockSpec((tile * MLA_ROPE // LANES, LANES), lambda i: (i, 0)), _const_spec(freq.shape), _const_spec(g.shape), _const_spec(w1.shape),
                  _const_spec(qn.shape), _const_spec(wuq.shape), _const_spec(wuqr.shape), _const_spec(kvn.shape),
                  _const_spec(wuk.shape), _const_spec(wuv.shape), _const_spec(ones.shape), mem_spec,
                  _const_spec(gm.shape), _const_spec(wm.shape)],
        out_specs=[row(c) for c in widths],
        out_shape=[jax.ShapeDtypeStruct((rows, c), BF16) for c in widths],
        scratch_shapes=[pltpu.VMEM((mem_len, 2 * _QM), BF16)],
        compiler_params=pltpu.CompilerParams(dimension_semantics=("arbitrary",), vmem_limit_bytes=VMEM_LIMIT),
        name="in_proj",
    )(x2d, pos2d, freq, g, w1, qn, wuq, wuqr, kvn, wuk, wuv, ones, mem, gm, wm)


def _key_rows(ref, j, tile):
    if isinstance(j, int):
        return ref[j * tile:(j + 1) * tile, :]
    return ref[pl.ds(pl.multiple_of(j * tile, tile), tile), :]


def _cat_rows(parts):
    return jnp.concatenate(parts, axis=0)


def _lane_fold_max(x):
    top = x[:, :LANES]
    for c0 in range(LANES, x.shape[1], LANES):
        top = jnp.maximum(top, x[:, c0:c0 + LANES])
    return jnp.max(top, axis=-1, keepdims=True)


def _mla_kernel(q_ref, k_ref, v_ref, o_ref):
    i = pl.program_id(2)
    heads = [slice(hh * LANES, (hh + 1) * LANES) for hh in range(q_ref.shape[1] // LANES)]

    def tile(j, carry, diag):
        ms, accs = carry
        kb = _key_rows(k_ref, j, Q_TILE)
        vb = _key_rows(v_ref, j, Q_TILE)
        scores = [_dot_t(q_ref[:, hs], kb[:, hs]) for hs in heads]
        if diag:
            r = lax.broadcasted_iota(jnp.int32, (Q_TILE, Q_TILE), 0)
            c = lax.broadcasted_iota(jnp.int32, (Q_TILE, Q_TILE), 1)
            scores = [jnp.where(r >= c, s, NEG_BIG) for s in scores]
        new_m, new_acc = [], []
        for hh, hs in enumerate(heads):
            m = jnp.maximum(ms[hh], _lane_fold_max(scores[hh]))
            p = jnp.exp2(scores[hh] - m).astype(BF16)
            new_m.append(m)
            new_acc.append(jnp.exp2(ms[hh] - m) * accs[hh] + _dot(p, vb[:, hs]))
        return tuple(new_m), tuple(new_acc)

    def run(n_full):
        col = jnp.full((Q_TILE, 1), NEG_BIG, F32)
        zero = jnp.zeros((Q_TILE, LANES), F32)
        carry = ((col,) * len(heads), (zero,) * len(heads))
        for j in range(n_full):
            carry = tile(j, carry, False)
        _, accs = tile(n_full, carry, True)
        first = lax.broadcasted_iota(jnp.int32, (Q_TILE, LANES), 1) < MLA_V
        half_turn = lambda x: pltpu.roll(x, MLA_V, 1)
        for pr in range(len(heads) // 2):
            even, odd = accs[2 * pr], accs[2 * pr + 1]
            o_ref[:, heads[pr]] = jnp.where(first, even / half_turn(even), half_turn(odd) / odd).astype(BF16)

    for n_full in range(k_ref.shape[0] // Q_TILE):
        pl.when(i == n_full)(functools.partial(run, n_full))


def _sb_kernel(q_ref, k_ref, v_ref, o_ref):
    i = pl.program_id(2)
    slots = [slice(sl * LANES, (sl + 1) * LANES) for sl in range(q_ref.shape[1] // LANES)]
    first = lax.broadcasted_iota(jnp.int32, (SB_TILE, LANES), 1) < SB_DIM
    r = lax.broadcasted_iota(jnp.int32, (2 * SB_TILE, SB_TILE), 0) & (SB_TILE - 1)
    c = lax.broadcasted_iota(jnp.int32, (2 * SB_TILE, SB_TILE), 1)
    strict = r > c
    later = jnp.where(strict[:SB_TILE], 1.0, 0.0).astype(BF16)
    q2 = []
    for sl in slots:
        qp = q_ref[:, sl]
        zq = jnp.zeros_like(qp)
        q2.append(_cat_rows([jnp.where(first, qp, zq), jnp.where(first, zq, qp)]))

    def tile(j, tots, accs, diag):
        kb = _key_rows(k_ref, j, SB_TILE)
        vb = _key_rows(v_ref, j, SB_TILE)
        logits = [_dot_t(q2[n], kb[:, sl]) for n, sl in enumerate(slots)]
        drops, keeps = [], []
        for z in logits:
            lg = jnp.log2(1.0 + jnp.exp2(-jnp.abs(z)))
            drop = lg - jnp.minimum(z, 0.0)
            keep = z + drop
            drops.append(drop)
            keeps.append(jnp.where(strict, keep, 0.0) if diag else keep)
        rights = [_dot(keep.astype(BF16), later) for keep in keeps]
        new_tots, new_accs = [], []
        for n, sl in enumerate(slots):
            w = jnp.exp2(-tots[n] - (drops[n] + rights[n]))
            if diag:
                w = jnp.where(strict, w, 0.0)
            new_tots.append(tots[n] + rights[n][:, 0:1] + keeps[n][:, 0:1])
            new_accs.append(accs[n] + _dot(w.astype(BF16), vb[:, sl]))
        return tuple(new_tots), tuple(new_accs)

    def any_live(tots):
        low = tots[0]
        for tot in tots[1:]:
            low = jnp.minimum(low, tot)
        return (jnp.min(low) < SB_DEAD_LOG2).astype(jnp.int32)

    def live(state):
        t, go, _, _ = state
        return jnp.logical_and(t < i, go > 0)

    def step(state):
        t, _, tots, accs = state
        tots, accs = tile(i - 1 - t, tots, accs, False)
        return t + 1, any_live(tots), tots, accs

    def run(n_older):
        zero_tot = jnp.zeros((2 * SB_TILE, 1), F32)
        zero_acc = jnp.zeros((2 * SB_TILE, LANES), F32)
        tots, accs = tile(i, (zero_tot,) * len(slots), (zero_acc,) * len(slots), True)
        for t in range(n_older):
            tots, accs = tile(i - 1 - t, tots, accs, False)
        if n_older:
            _, _, _, accs = lax.while_loop(live, step, (jnp.int32(n_older), any_live(tots), tots, accs))
        for n, sl in enumerate(slots):
            o_ref[:, sl] = jnp.where(first, accs[n][:SB_TILE], accs[n][SB_TILE:]).astype(BF16)

    pl.when(i == 0)(functools.partial(run, 0))
    pl.when(i > 0)(functools.partial(run, 1))


def _causal_mixer(body, name, tile, pairs, q, k, v, qw, kw, vw, ow):
    b, s, _ = q.shape
    groups = v.shape[2] // (vw * pairs)
    qw, kw, vw, ow = qw * pairs, kw * pairs, vw * pairs, ow * pairs
    return pl.pallas_call(
        body,
        grid=(b, groups, s // tile),
        in_specs=[pl.BlockSpec((None, tile, qw), lambda bi, g, i: (bi, i, g)),
                  pl.BlockSpec((None, s, kw), lambda bi, g, i: (bi, 0, g)),
                  pl.BlockSpec((None, s, vw), lambda bi, g, i: (bi, 0, g))],
        out_specs=pl.BlockSpec((None, tile, ow), lambda bi, g, i: (bi, i, g)),
        out_shape=jax.ShapeDtypeStruct((b, s, groups * ow), BF16),
        compiler_params=pltpu.CompilerParams(dimension_semantics=("parallel", "parallel", "arbitrary"),
                                             vmem_limit_bytes=VMEM_LIMIT),
        name=name,
    )(q, k, v)


def _post_kernel(x_ref, oa_ref, ob_ref, om_ref, g_pre_ref, wg_ref, bg_ref, wbo_ref, wout_ref, g_post_ref,
                 g_mlp_ref, wup_ref, wdn_ref, g_mlp_post_ref, o_ref):
    x = x_ref[...]
    d = x.shape[1]
    h = _rms(x, g_pre_ref[...]).astype(BF16)
    merged = None
    for br, o_br in enumerate((oa_ref, ob_ref, om_ref)):
        sl = slice(br * d, (br + 1) * d)
        gate = jax.nn.sigmoid(_dot(h, wg_ref[:, sl]) + bg_ref[:, sl])
        yb = gate * _dot(o_br[...], wbo_ref[br])
        merged = yb if merged is None else merged + yb
    y = _dot(merged.astype(BF16), wout_ref[...])
    x1 = x + _rms(y, g_post_ref[...])

    h2 = _rms(x1, g_mlp_ref[...]).astype(BF16)
    down = None
    for c0 in range(0, wup_ref.shape[1], FF_CHUNK):
        u = jnp.maximum(_dot(h2, wup_ref[:, c0:c0 + FF_CHUNK]), 0.0)
        part = _dot((u * u).astype(BF16), wdn_ref[c0:c0 + FF_CHUNK, :])
        down = part if down is None else down + part
    o_ref[...] = x1 + _rms(down, g_mlp_post_ref[...])


def _post(x2d, oa, ob, om, g_pre, wg, bg, wbo, wout, g_post, g_mlp, wup, wdn, g_mlp_post):
    rows, d = x2d.shape
    tile = POST_TILE
    row = lambda c: pl.BlockSpec((tile, c), lambda i: (i, 0))
    consts = (g_pre, wg, bg, wbo, wout, g_post, g_mlp, wup, wdn, g_mlp_post)
    return pl.pallas_call(
        _post_kernel,
        grid=(rows // tile,),
        in_specs=[row(d), row(oa.shape[1]), row(ob.shape[1]), row(om.shape[1])] + [_const_spec(a.shape) for a in consts],
        out_specs=row(d),
        out_shape=jax.ShapeDtypeStruct((rows, d), F32),
        compiler_params=pltpu.CompilerParams(dimension_semantics=("parallel",), vmem_limit_bytes=VMEM_LIMIT),
        name="post",
    )(x2d, oa, ob, om, *consts)


def _pad_heads(w, heads, width):
    r = w.shape[0]
    w = w.reshape(r, heads, width)
    return jnp.pad(w, ((0, 0), (0, 0), (0, LANES - width))).reshape(r, heads * LANES)


def _rotate_half_cols(w_rope):
    half = MLA_ROPE // 2
    return jnp.concatenate([-w_rope[..., half:], w_rope[..., :half]], axis=-1)


def kernel(x, mem, positions, ln_mix_pre, w_in, b_gate, q_norm, w_uq, kv_norm, w_uk, w_uv, mem_norm, w_mem_kv,
           w_branch_out, w_out, ln_mix_post, ln_mlp_pre, w_mlp_up, w_mlp_down, ln_mlp_post):
    b, s, d = x.shape
    m = mem.shape[1]
    depth = w_in.shape[0]
    half = MLA_ROPE // 2
    inv_freq = 1.0 / (ROPE_THETA ** (jnp.arange(half, dtype=F32) * (2.0 / MLA_ROPE)))
    groups = LANES // MLA_ROPE
    freq = jnp.tile(jnp.concatenate([inv_freq, inv_freq]), groups).reshape(1, LANES)
    ones = jnp.tile((jnp.arange(LANES) >= MLA_V).astype(F32), MLA_HEADS).reshape(1, -1)
    pos2d = positions.astype(F32).reshape(b * s // IN_TILE, groups, IN_TILE // groups).swapaxes(1, 2)
    pos2d = jnp.repeat(pos2d, MLA_ROPE, axis=2).reshape(b * s // groups, LANES)
    x2d = x.reshape(b * s, d)
    row = lambda a: a.reshape(1, -1)

    for l in range(depth):
        cq_w, ckv_w, kr_w, sb_w, qm_w, gate_w = jnp.split(
            w_in[l].astype(BF16), np.cumsum([_CQ, _CKV, MLA_ROPE, _SB, _QM]).tolist(), axis=1)
        gap = jnp.zeros((d, MLA_ROPE), BF16)
        w1 = jnp.concatenate([cq_w, ckv_w, sb_w, qm_w, _rotate_half_cols(kr_w), gap, kr_w, gap,
                              jnp.zeros((d, _W1_PAD), BF16)], axis=1)
        uq = w_uq[l].reshape(-1, MLA_HEADS, MLA_NOPE + MLA_ROPE)
        wuq = _pad_heads(w_uq[l], MLA_HEADS, MLA_NOPE + MLA_ROPE).astype(BF16)
        wuqr = _rotate_half_cols(uq[..., MLA_NOPE:]).reshape(-1, MLA_HEADS * MLA_ROPE).astype(BF16)
        wuk = _pad_heads(w_uk[l], MLA_HEADS, MLA_NOPE).astype(BF16)

        q, k, v, sq, sk, sv, o_mem = _in_proj(
            x2d, pos2d, freq, row(ln_mix_pre[l]), w1, row(q_norm[l]), wuq, wuqr, row(kv_norm[l]), wuk,
            _pad_heads(w_uv[l], MLA_HEADS, MLA_V).astype(BF16), ones, mem, row(mem_norm[l]),
            w_mem_kv[l].astype(BF16), s)

        seq = lambda a: a.reshape(b, s, -1)
        o_mla = _causal_mixer(_mla_kernel, "mla_attn", Q_TILE, MLA_PAIRS, seq(q), seq(k), seq(v),
                              2 * LANES, 2 * LANES, 2 * LANES, LANES)
        o_sb = _causal_mixer(_sb_kernel, "sb_attn", SB_TILE, SB_PAIRS, seq(sq), seq(sk), seq(sv),
                             LANES, LANES, LANES, LANES)

        x2d = _post(x2d, o_mla.reshape(b * s, -1), o_sb.reshape(b * s, -1), o_mem, row(ln_mix_pre[l]),
                    gate_w, row(b_gate[l]), w_branch_out[l].astype(BF16), w_out[l].astype(BF16),
                    row(ln_mix_post[l]), row(ln_mlp_pre[l]), w_mlp_up[l].astype(BF16), w_mlp_down[l].astype(BF16),
                    row(ln_mlp_post[l]))
    return x2d.reshape(b, s, d)
```

```python
import functools

import jax
import jax.numpy as jnp
import numpy as np
from jax import lax
from jax.experimental import pallas as pl
from jax.experimental.pallas import tpu as pltpu

F32 = jnp.float32
BF16 = jnp.bfloat16

LANES = 128
EPS = 1e-6
ROPE_THETA = 10000.0

MLA_HEADS, MLA_NOPE, MLA_ROPE, MLA_V = 8, 64, 32, 64
SB_HEADS, SB_DIM = 8, 64
MEM_HEADS, MEM_DIM = 4, 128
N_BRANCH = 3
NEG_BIG = -1e30

IN_TILE, POST_TILE = 1024, 512
Q_TILE = 512
SB_TILE = 256
MLA_PAIRS, SB_PAIRS = 2, 4
SB_DEAD_LOG2 = 160.0
LOG2E = 1.4426950408889634
FF_CHUNK = 1024
VMEM_LIMIT = 60 * 1024 * 1024


def _rms(x, g):
    return x * lax.rsqrt(jnp.mean(x * x, axis=-1, keepdims=True) + EPS) * g


def _dot(a, b):
    return jnp.dot(a, b, preferred_element_type=F32)


def _dot_t(a, b):
    return lax.dot_general(a, b, (((1,), (1,)), ((), ())), preferred_element_type=F32)


def _const_spec(shape):
    nd = len(shape)
    return pl.BlockSpec(shape, lambda *_: (0,) * nd, pipeline_mode=pl.Buffered(1))


def _mem_kv_kernel(mem_ref, g_ref, w_ref, o_ref):
    h = _rms(mem_ref[...], g_ref[...]).astype(BF16)
    o_ref[...] = _dot(h, w_ref[...]).astype(BF16)


def _mem_kv(mem2d, g, w):
    rows, d = mem2d.shape
    n = w.shape[1]
    tile = 512
    return pl.pallas_call(
        _mem_kv_kernel,
        grid=(rows // tile,),
        in_specs=[pl.BlockSpec((tile, d), lambda i: (i, 0)), _const_spec((1, d)), _const_spec((d, n))],
        out_specs=pl.BlockSpec((tile, n), lambda i: (i, 0)),
        out_shape=jax.ShapeDtypeStruct((rows, n), BF16),
        compiler_params=pltpu.CompilerParams(dimension_semantics=("parallel",), vmem_limit_bytes=VMEM_LIMIT),
        name="mem_kv",
    )(mem2d, g, w)


_CQ, _CKV, _SB, _QM, _KR = 384, 256, 3 * SB_HEADS * SB_DIM, MEM_HEADS * MEM_DIM, LANES
_O_CKV = _CQ
_O_SB = _O_CKV + _CKV
_O_QM = _O_SB + _SB
_O_KR = _O_QM + _QM
_W1 = _O_KR + _KR
MXU_WIDTH = 256
_W1_PAD = -_W1 % MXU_WIDTH


def _in_proj_kernel(x_ref, pos_ref, freq_ref, g_ref, w1_ref, qn_ref, wuq_ref, wuqr_ref, kvn_ref, wuk_ref,
                    wuv_ref, ones_ref, mkv_ref, q_out, k_out, v_out, sq_out, sk_out, sv_out, om_out):
    groups = LANES // MLA_ROPE
    blk = pos_ref.shape[0]
    ang = pos_ref[...] * freq_ref[...]
    cos_packed, sin_packed = jnp.cos(ang), jnp.sin(ang)
    lane = lax.broadcasted_iota(jnp.int32, (blk, LANES), 1)
    rotary = jnp.logical_and(lane >= MLA_NOPE, lane < MLA_NOPE + MLA_ROPE)
    cos_blocks, sin_blocks = [], []
    for gi in range(groups):
        shift = (MLA_NOPE - gi * MLA_ROPE) % LANES
        place = (lambda t: t) if shift == 0 else (lambda t: pltpu.roll(t, shift, 1))
        cos_blocks.append(jnp.where(rotary, place(cos_packed), 1.0))
        sin_blocks.append(jnp.where(rotary, place(sin_packed), 0.0))
    cos, sin = jnp.concatenate(cos_blocks, axis=0), jnp.concatenate(sin_blocks, axis=0)

    h = _rms(x_ref[...], g_ref[...]).astype(BF16)
    proj = _dot(h, w1_ref[...])

    cq = _rms(proj[:, :_CQ], qn_ref[...]).astype(BF16)
    q = _dot(cq, wuq_ref[...])
    qr = _dot(cq, wuqr_ref[...])
    ckv = _rms(proj[:, _O_CKV:_O_SB], kvn_ref[...]).astype(BF16)
    kn = _dot(ckv, wuk_ref[...])
    v_out[...] = (_dot(ckv, wuv_ref[...]) + ones_ref[...]).astype(BF16)
    kr_pair = proj[:, _O_KR:_W1]
    lane_full = lax.broadcasted_iota(jnp.int32, kr_pair.shape, 1)
    on_rotary = jnp.logical_and(lane_full >= MLA_NOPE, lane_full < MLA_NOPE + MLA_ROPE)
    kr = jnp.where(on_rotary, kr_pair * cos + pltpu.roll(kr_pair, MLA_NOPE, 1) * sin, 0.0)
    q_scale = (MLA_NOPE + MLA_ROPE) ** -0.5 * LOG2E
    for hd in range(MLA_HEADS):
        sl = slice(hd * LANES, (hd + 1) * LANES)
        at = hd * MLA_ROPE
        qr_slot = qr[:, at // LANES * LANES:(at // LANES + 1) * LANES]
        shift = (MLA_NOPE - at % LANES) % LANES
        qr_head = qr_slot if shift == 0 else pltpu.roll(qr_slot, shift, 1)
        q_out[:, sl] = ((q[:, sl] * cos + qr_head * sin) * q_scale).astype(BF16)
        k_out[:, sl] = (kn[:, sl] + kr).astype(BF16)

    w = SB_HEADS * SB_DIM
    sq_out[...] = (proj[:, _O_SB:_O_SB + w] * (SB_DIM ** -0.5 * LOG2E)).astype(BF16)
    sk_out[...] = proj[:, _O_SB + w:_O_SB + 2 * w].astype(BF16)
    sv_out[...] = proj[:, _O_SB + 2 * w:_O_QM].astype(BF16)

    for hd in range(MEM_HEADS):
        sl = slice(hd * MEM_DIM, (hd + 1) * MEM_DIM)
        vsl = slice(_QM + hd * MEM_DIM, _QM + (hd + 1) * MEM_DIM)
        qm = (proj[:, _O_QM + hd * MEM_DIM:_O_QM + (hd + 1) * MEM_DIM] * (MEM_DIM ** -0.5)).astype(BF16)
        s = _dot_t(qm, mkv_ref[:, sl])
        p = jnp.exp(s - jnp.max(s, axis=-1, keepdims=True))
        o = _dot(p.astype(BF16), mkv_ref[:, vsl])
        om_out[:, sl] = (o / jnp.sum(p, axis=-1, keepdims=True)).astype(BF16)


def _in_proj(x2d, pos2d, freq, g, w1, qn, wuq, wuqr, kvn, wuk, wuv, ones, mkv, seq):
    rows, d = x2d.shape
    tile = IN_TILE
    per_seq = seq // tile
    mem_len = mkv.shape[1]
    row = lambda c: pl.BlockSpec((tile, c), lambda i: (i, 0))
    widths = (MLA_HEADS * LANES, MLA_HEADS * LANES, MLA_HEADS * LANES, SB_HEADS * SB_DIM, SB_HEADS * SB_DIM,
              SB_HEADS * SB_DIM, MEM_HEADS * MEM_DIM)
    mem_spec = pl.BlockSpec((None, mem_len, 2 * _QM), lambda i: (i // per_seq, 0, 0))
    return pl.pallas_call(
        _in_proj_kernel,
        grid=(rows // tile,),
        in_specs=[row(d), pl.BlockSpec((tile * MLA_ROPE // LANES, LANES), lambda i: (i, 0)), _const_spec(freq.shape), _const_spec(g.shape), _const_spec(w1.shape),
                  _const_spec(qn.shape), _const_spec(wuq.shape), _const_spec(wuqr.shape), _const_spec(kvn.shape),
                  _const_spec(wuk.shape), _const_spec(wuv.shape), _const_spec(ones.shape), mem_spec],
        out_specs=[row(c) for c in widths],
        out_shape=[jax.ShapeDtypeStruct((rows, c), BF16) for c in widths],
        compiler_params=pltpu.CompilerParams(dimension_semantics=("parallel",), vmem_limit_bytes=VMEM_LIMIT),
        name="in_proj",
    )(x2d, pos2d, freq, g, w1, qn, wuq, wuqr, kvn, wuk, wuv, ones, mkv)


def _key_rows(ref, j, tile):
    if isinstance(j, int):
        return ref[j * tile:(j + 1) * tile, :]
    return ref[pl.ds(pl.multiple_of(j * tile, tile), tile), :]


def _cat_rows(parts):
    return jnp.concatenate(parts, axis=0)


def _lane_fold_max(x):
    top = x[:, :LANES]
    for c0 in range(LANES, x.shape[1], LANES):
        top = jnp.maximum(top, x[:, c0:c0 + LANES])
    return jnp.max(top, axis=-1, keepdims=True)


def _mla_kernel(q_ref, k_ref, v_ref, o_ref):
    i = pl.program_id(2)
    heads = [slice(hh * LANES, (hh + 1) * LANES) for hh in range(q_ref.shape[1] // LANES)]

    def tile(j, carry, diag):
        ms, accs = carry
        kb = _key_rows(k_ref, j, Q_TILE)
        vb = _key_rows(v_ref, j, Q_TILE)
        scores = [_dot_t(q_ref[:, hs], kb[:, hs]) for hs in heads]
        if diag:
            r = lax.broadcasted_iota(jnp.int32, (Q_TILE, Q_TILE), 0)
            c = lax.broadcasted_iota(jnp.int32, (Q_TILE, Q_TILE), 1)
            scores = [jnp.where(r >= c, s, NEG_BIG) for s in scores]
        new_m, new_acc = [], []
        for hh, hs in enumerate(heads):
            m = jnp.maximum(ms[hh], _lane_fold_max(scores[hh]))
            p = jnp.exp2((scores[hh] - m).astype(BF16))
            new_m.append(m)
            new_acc.append(jnp.exp2(ms[hh] - m) * accs[hh] + _dot(p, vb[:, hs]))
        return tuple(new_m), tuple(new_acc)

    def run(n_full):
        col = jnp.full((Q_TILE, 1), NEG_BIG, F32)
        zero = jnp.zeros((Q_TILE, LANES), F32)
        carry = ((col,) * len(heads), (zero,) * len(heads))
        for j in range(n_full):
            carry = tile(j, carry, False)
        _, accs = tile(n_full, carry, True)
        first = lax.broadcasted_iota(jnp.int32, (Q_TILE, LANES), 1) < MLA_V
        half_turn = lambda x: pltpu.roll(x, MLA_V, 1)
        for pr in range(len(heads) // 2):
            even, odd = accs[2 * pr], accs[2 * pr + 1]
            o_ref[:, heads[pr]] = jnp.where(first, even / half_turn(even), half_turn(odd) / odd).astype(BF16)

    for n_full in range(k_ref.shape[0] // Q_TILE):
        pl.when(i == n_full)(functools.partial(run, n_full))


def _sb_kernel(q_ref, k_ref, v_ref, o_ref):
    i = pl.program_id(2)
    slots = [slice(sl * LANES, (sl + 1) * LANES) for sl in range(q_ref.shape[1] // LANES)]
    first = lax.broadcasted_iota(jnp.int32, (SB_TILE, LANES), 1) < SB_DIM
    r = lax.broadcasted_iota(jnp.int32, (2 * SB_TILE, SB_TILE), 0) & (SB_TILE - 1)
    c = lax.broadcasted_iota(jnp.int32, (2 * SB_TILE, SB_TILE), 1)
    strict = r > c
    later = jnp.where(strict[:SB_TILE], 1.0, 0.0).astype(BF16)
    q2 = []
    for sl in slots:
        qp = q_ref[:, sl]
        zq = jnp.zeros_like(qp)
        q2.append(_cat_rows([jnp.where(first, qp, zq), jnp.where(first, zq, qp)]))

    def tile(j, tots, accs, diag):
        kb = _key_rows(k_ref, j, SB_TILE)
        vb = _key_rows(v_ref, j, SB_TILE)
        logits = [_dot_t(q2[n], kb[:, sl]) for n, sl in enumerate(slots)]
        drops, keeps = [], []
        for z in logits:
            lg = jnp.log2(1.0 + jnp.exp2(-jnp.abs(z)))
            drop = lg - jnp.minimum(z, 0.0)
            keep = z + drop
            drops.append(drop)
            keeps.append(jnp.where(strict, keep, 0.0) if diag else keep)
        rights = [_dot(keep.astype(BF16), later) for keep in keeps]
        new_tots, new_accs = [], []
        for n, sl in enumerate(slots):
            w = jnp.exp2(-tots[n] - (drops[n] + rights[n]))
            if diag:
                w = jnp.where(strict, w, 0.0)
            new_tots.append(tots[n] + rights[n][:, 0:1] + keeps[n][:, 0:1])
            new_accs.append(accs[n] + _dot(w.astype(BF16), vb[:, sl]))
        return tuple(new_tots), tuple(new_accs)

    def any_live(tots):
        low = tots[0]
        for tot in tots[1:]:
            low = jnp.minimum(low, tot)
        return (jnp.min(low) < SB_DEAD_LOG2).astype(jnp.int32)

    def live(state):
        t, go, _, _ = state
        return jnp.logical_and(t < i, go > 0)

    def step(state):
        t, _, tots, accs = state
        tots, accs = tile(i - 1 - t, tots, accs, False)
        return t + 1, any_live(tots), tots, accs

    def run(n_older):
        zero_tot = jnp.zeros((2 * SB_TILE, 1), F32)
        zero_acc = jnp.zeros((2 * SB_TILE, LANES), F32)
        tots, accs = tile(i, (zero_tot,) * len(slots), (zero_acc,) * len(slots), True)
        for t in range(n_older):
            tots, accs = tile(i - 1 - t, tots, accs, False)
        if n_older:
            _, _, _, accs = lax.while_loop(live, step, (jnp.int32(n_older), any_live(tots), tots, accs))
        for n, sl in enumerate(slots):
            o_ref[:, sl] = jnp.where(first, accs[n][:SB_TILE], accs[n][SB_TILE:]).astype(BF16)

    pl.when(i == 0)(functools.partial(run, 0))
    pl.when(i > 0)(functools.partial(run, 1))


def _causal_mixer(body, name, tile, pairs, q, k, v, qw, kw, vw, ow):
    b, s, _ = q.shape
    groups = v.shape[2] // (vw * pairs)
    qw, kw, vw, ow = qw * pairs, kw * pairs, vw * pairs, ow * pairs
    return pl.pallas_call(
        body,
        grid=(b, groups, s // tile),
        in_specs=[pl.BlockSpec((None, tile, qw), lambda bi, g, i: (bi, i, g)),
                  pl.BlockSpec((None, s, kw), lambda bi, g, i: (bi, 0, g)),
                  pl.BlockSpec((None, s, vw), lambda bi, g, i: (bi, 0, g))],
        out_specs=pl.BlockSpec((None, tile, ow), lambda bi, g, i: (bi, i, g)),
        out_shape=jax.ShapeDtypeStruct((b, s, groups * ow), BF16),
        compiler_params=pltpu.CompilerParams(dimension_semantics=("parallel", "parallel", "arbitrary"),
                                             vmem_limit_bytes=VMEM_LIMIT),
        name=name,
    )(q, k, v)


def _post_kernel(x_ref, oa_ref, ob_ref, om_ref, g_pre_ref, wg_ref, bg_ref, wbo_ref, wout_ref, g_post_ref,
                 g_mlp_ref, wup_ref, wdn_ref, g_mlp_post_ref, o_ref):
    x = x_ref[...]
    d = x.shape[1]
    h = _rms(x, g_pre_ref[...]).astype(BF16)
    merged = None
    for br, o_br in enumerate((oa_ref, ob_ref, om_ref)):
        sl = slice(br * d, (br + 1) * d)
        gate = jax.nn.sigmoid(_dot(h, wg_ref[:, sl]) + bg_ref[:, sl])
        yb = gate * _dot(o_br[...], wbo_ref[br])
        merged = yb if merged is None else merged + yb
    y = _dot(merged.astype(BF16), wout_ref[...])
    x1 = x + _rms(y, g_post_ref[...])

    h2 = _rms(x1, g_mlp_ref[...]).astype(BF16)
    down = None
    for c0 in range(0, wup_ref.shape[1], FF_CHUNK):
        u = jnp.maximum(_dot(h2, wup_ref[:, c0:c0 + FF_CHUNK]), 0.0)
        part = _dot((u * u).astype(BF16), wdn_ref[c0:c0 + FF_CHUNK, :])
        down = part if down is None else down + part
    o_ref[...] = x1 + _rms(down, g_mlp_post_ref[...])


def _post(x2d, oa, ob, om, g_pre, wg, bg, wbo, wout, g_post, g_mlp, wup, wdn, g_mlp_post):
    rows, d = x2d.shape
    tile = POST_TILE
    row = lambda c: pl.BlockSpec((tile, c), lambda i: (i, 0))
    consts = (g_pre, wg, bg, wbo, wout, g_post, g_mlp, wup, wdn, g_mlp_post)
    return pl.pallas_call(
        _post_kernel,
        grid=(rows // tile,),
        in_specs=[row(d), row(oa.shape[1]), row(ob.shape[1]), row(om.shape[1])] + [_const_spec(a.shape) for a in consts],
        out_specs=row(d),
        out_shape=jax.ShapeDtypeStruct((rows, d), F32),
        compiler_params=pltpu.CompilerParams(dimension_semantics=("parallel",), vmem_limit_bytes=VMEM_LIMIT),
        name="post",
    )(x2d, oa, ob, om, *consts)


def _pad_heads(w, heads, width):
    r = w.shape[0]
    w = w.reshape(r, heads, width)
    return jnp.pad(w, ((0, 0), (0, 0), (0, LANES - width))).reshape(r, heads * LANES)


def _rotate_half_cols(w_rope):
    half = MLA_ROPE // 2
    return jnp.concatenate([-w_rope[..., half:], w_rope[..., :half]], axis=-1)


def kernel(x, mem, positions, ln_mix_pre, w_in, b_gate, q_norm, w_uq, kv_norm, w_uk, w_uv, mem_norm, w_mem_kv,
           w_branch_out, w_out, ln_mix_post, ln_mlp_pre, w_mlp_up, w_mlp_down, ln_mlp_post):
    b, s, d = x.shape
    m = mem.shape[1]
    depth = w_in.shape[0]
    half = MLA_ROPE // 2
    inv_freq = 1.0 / (ROPE_THETA ** (jnp.arange(half, dtype=F32) * (2.0 / MLA_ROPE)))
    groups = LANES // MLA_ROPE
    freq = jnp.tile(jnp.concatenate([inv_freq, inv_freq]), groups).reshape(1, LANES)
    ones = jnp.tile((jnp.arange(LANES) >= MLA_V).astype(F32), MLA_HEADS).reshape(1, -1)
    pos2d = positions.astype(F32).reshape(b * s // IN_TILE, groups, IN_TILE // groups).swapaxes(1, 2)
    pos2d = jnp.repeat(pos2d, MLA_ROPE, axis=2).reshape(b * s // groups, LANES)
    mem2d = mem.reshape(b * m, d)
    x2d = x.reshape(b * s, d)
    row = lambda a: a.reshape(1, -1)

    for l in range(depth):
        cq_w, ckv_w, kr_w, sb_w, qm_w, gate_w = jnp.split(
            w_in[l].astype(BF16), np.cumsum([_CQ, _CKV, MLA_ROPE, _SB, _QM]).tolist(), axis=1)
        gap = jnp.zeros((d, MLA_ROPE), BF16)
        w1 = jnp.concatenate([cq_w, ckv_w, sb_w, qm_w, _rotate_half_cols(kr_w), gap, kr_w, gap,
                              jnp.zeros((d, _W1_PAD), BF16)], axis=1)
        uq = w_uq[l].reshape(-1, MLA_HEADS, MLA_NOPE + MLA_ROPE)
        wuq = _pad_heads(w_uq[l], MLA_HEADS, MLA_NOPE + MLA_ROPE).astype(BF16)
        wuqr = _rotate_half_cols(uq[..., MLA_NOPE:]).reshape(-1, MLA_HEADS * MLA_ROPE).astype(BF16)
        wuk = _pad_heads(w_uk[l], MLA_HEADS, MLA_NOPE).astype(BF16)

        mkv = _mem_kv(mem2d, row(mem_norm[l]), w_mem_kv[l].astype(BF16))
        mkv = mkv.reshape(b, m, 2 * MEM_HEADS * MEM_DIM)

        q, k, v, sq, sk, sv, o_mem = _in_proj(
            x2d, pos2d, freq, row(ln_mix_pre[l]), w1, row(q_norm[l]), wuq, wuqr, row(kv_norm[l]), wuk,
            _pad_heads(w_uv[l], MLA_HEADS, MLA_V).astype(BF16), ones, mkv, s)

        seq = lambda a: a.reshape(b, s, -1)
        o_mla = _causal_mixer(_mla_kernel, "mla_attn", Q_TILE, MLA_PAIRS, seq(q), seq(k), seq(v),
                              2 * LANES, 2 * LANES, 2 * LANES, LANES)
        o_sb = _causal_mixer(_sb_kernel, "sb_attn", SB_TILE, SB_PAIRS, seq(sq), seq(sk), seq(sv),
                             LANES, LANES, LANES, LANES)

        x2d = _post(x2d, o_mla.reshape(b * s, -1), o_sb.reshape(b * s, -1), o_mem, row(ln_mix_pre[l]),
                    gate_w, row(b_gate[l]), w_branch_out[l].astype(BF16), w_out[l].astype(BF16),
                    row(ln_mix_post[l]), row(ln_mlp_pre[l]), w_mlp_up[l].astype(BF16), w_mlp_down[l].astype(BF16),
                    row(ln_mlp_post[l]))
    return x2d.reshape(b, s, d)
```
